```python
import math
import jax, jax.numpy as jnp
from jax import lax
import numpy as np

D_MODEL = 1024
BATCH = 8
SEQ = 4096
DEPTH = 4

HEAD_DIM = 64
NSA_HEADS = 8
NSA_GROUPS = 2
CMP_BLOCK = 32
CMP_STRIDE = 16
CMP_HIDDEN = 256
SEL_BLOCK = 64
SEL_TOPK = 8
NSA_WINDOW = 512
SWA_HEADS = 8
SWA_GROUPS = 2
SWA_WINDOW = 128
FOX_HEADS = 8
FOX_GATE_BIAS = 3.0
Q_BLOCK = 128
REL_BUCKETS = 32
REL_MAX_DIST = 128
D_FF = 2816
N_BRANCH = 3
LN_EPS = 1e-5
DN_ALPHA = (2 * DEPTH) ** 0.25
DN_BETA = (8 * DEPTH) ** -0.25

W_A = NSA_HEADS * HEAD_DIM
KV_A = NSA_GROUPS * HEAD_DIM
W_B = SWA_HEADS * HEAD_DIM
KV_B = SWA_GROUPS * HEAD_DIM
W_C = FOX_HEADS * HEAD_DIM
D_IN = W_A + 6 * KV_A + 3 * NSA_HEADS + W_B + 2 * KV_B + 3 * W_C + FOX_HEADS

kernel_name = 'hybrid_nsa_swa_sink_fox_macaron_deepnorm'


def _layer_norm(x, g, b):
    xf = x.astype(jnp.float32)
    mu = jnp.mean(xf, axis=-1, keepdims=True)
    var = jnp.mean(jnp.square(xf - mu), axis=-1, keepdims=True)
    return ((xf - mu) * lax.rsqrt(var + LN_EPS) * g + b).astype(x.dtype)


def _swiglu(x, w1, w2):
    gt, up = jnp.split(x @ w1, 2, axis=-1)
    return (jax.nn.silu(gt) * up) @ w2


def _t5_bucket(dist):
    n = jnp.maximum(dist, 0)
    exact = REL_BUCKETS // 2
    log_ratio = jnp.log(jnp.maximum(n, 1).astype(jnp.float32) / exact) / math.log(REL_MAX_DIST / exact)
    large = exact + (log_ratio * (REL_BUCKETS - exact)).astype(jnp.int32)
    return jnp.where(n < exact, n, jnp.minimum(large, REL_BUCKETS - 1))


def _masked_softmax(logits, mask):
    logits = jnp.where(mask, logits.astype(jnp.float32), -jnp.inf)
    m = jnp.max(logits, axis=-1, keepdims=True)
    m = jnp.where(jnp.isfinite(m), m, 0.0)
    p = jnp.exp(logits - m)
    return p / jnp.maximum(jnp.sum(p, axis=-1, keepdims=True), 1e-30)


def _split_cols(z):
    sizes = [W_A, 6 * KV_A, 3 * NSA_HEADS, W_B, KV_B, KV_B, W_C, W_C, W_C, FOX_HEADS]
    return jnp.split(z, np.cumsum(sizes)[:-1].tolist(), axis=-1)


def _compress(z, pe, w1, w2):
    B, S, G, dh = z.shape
    r = CMP_BLOCK // CMP_STRIDE
    n_chunk = S // CMP_STRIDE
    nc = n_chunk - r + 1
    zr = z.reshape(B, n_chunk, CMP_STRIDE, G, dh)
    blocks = jnp.concatenate([zr[:, j:j + nc] for j in range(r)], axis=2)
    blocks = blocks + pe[:, None, :]
    flat = blocks.transpose(0, 3, 1, 2, 4).reshape(B, G, nc, CMP_BLOCK * dh)
    return jax.nn.gelu(flat @ w1) @ w2


def _nsa(q_a, kv_a, g_a, pe_k, pe_v, ck_w1, ck_w2, cv_w1, cv_w2, rel_a):
    B, S, _ = q_a.shape
    G, HPG, dh = NSA_GROUPS, NSA_HEADS // NSA_GROUPS, HEAD_DIM
    nb = S // Q_BLOCK
    scale = HEAD_DIM ** -0.5
    q = q_a.reshape(B, S, G, HPG, dh).transpose(0, 2, 3, 1, 4)
    k_cmp, v_cmp, k_slc, v_slc, k_win, v_win = [z.reshape(B, S, G, dh) for z in jnp.split(kv_a, 6, axis=-1)]
    kc = _compress(k_cmp, pe_k, ck_w1, ck_w2)
    vc = _compress(v_cmp, pe_v, cv_w1, cv_w2)
    nc = kc.shape[2]
    ns = S // SEL_BLOCK
    n_sel = min(SEL_TOPK, ns)
    ks = k_slc.reshape(B, ns, SEL_BLOCK, G, dh).transpose(0, 3, 1, 2, 4)
    vs = v_slc.reshape(B, ns, SEL_BLOCK, G, dh).transpose(0, 3, 1, 2, 4)
    pad = ((0, 0), (0, 0), (NSA_WINDOW, 0), (0, 0))
    kw = jnp.pad(k_win.transpose(0, 2, 1, 3), pad)
    vw = jnp.pad(v_win.transpose(0, 2, 1, 3), pad)
    gates = jax.nn.sigmoid(g_a.astype(jnp.float32)).reshape(B, S, G, HPG, 3).transpose(0, 2, 3, 1, 4)
    rel_g = rel_a.reshape(G, HPG, REL_BUCKETS)
    c_start = jnp.arange(nc) * CMP_STRIDE
    c_end = c_start + (CMP_BLOCK - 1)
    s_start = jnp.arange(ns) * SEL_BLOCK
    overlap = ((c_start[:, None] < s_start[None] + SEL_BLOCK) & (c_end[:, None] >= s_start[None])).astype(jnp.float32)
    i_w = jnp.arange(Q_BLOCK)
    j_w = jnp.arange(Q_BLOCK + NSA_WINDOW)
    dist_w = NSA_WINDOW + i_w[:, None] - j_w[None]
    band_w = (dist_w >= 0) & (dist_w < NSA_WINDOW)
    bias_w = rel_g[:, :, _t5_bucket(dist_w)]
    blk = jnp.arange(ns)
    sel_off = jnp.arange(SEL_BLOCK)
    gather = jax.vmap(jax.vmap(lambda kb, ix: kb[ix]))
    lookup = jax.vmap(lambda tab, bk: jnp.moveaxis(tab[:, bk], 0, 1), in_axes=(0, 1), out_axes=1)

    def block(qb):
        q0 = qb * Q_BLOCK
        t = q0 + jnp.arange(Q_BLOCK)
        qq = lax.dynamic_slice_in_dim(q, q0, Q_BLOCK, axis=3)
        gg = lax.dynamic_slice_in_dim(gates, q0, Q_BLOCK, axis=3)
        s_c = jnp.einsum('bghqd,bgkd->bghqk', qq, kc, preferred_element_type=jnp.float32) * scale
        s_c = s_c + rel_g[:, :, _t5_bucket(t[:, None] - c_end[None])]
        p_c = _masked_softmax(s_c, c_end[None] <= t[:, None])
        o_c = jnp.einsum('bghqk,bgkd->bghqd', p_c.astype(vc.dtype), vc)
        imp = jnp.einsum('bghqk,kj->bgqj', p_c, overlap)
        cur = (t // SEL_BLOCK)[:, None]
        forced = (blk[None] == 0) | (blk[None] == cur) | (blk[None] == cur - 1)
        valid = blk[None] * SEL_BLOCK <= t[:, None]
        score = jnp.where(forced, jnp.inf, jnp.where(valid, imp, -jnp.inf))
        _, idx = lax.top_k(score, n_sel)
        k_sel = gather(ks, idx).reshape(B, G, Q_BLOCK, n_sel * SEL_BLOCK, dh)
        v_sel = gather(vs, idx).reshape(B, G, Q_BLOCK, n_sel * SEL_BLOCK, dh)
        pos = (idx[..., None] * SEL_BLOCK + sel_off).reshape(B, G, Q_BLOCK, n_sel * SEL_BLOCK)
        dist = t[:, None] - pos
        s_s = jnp.einsum('bghqd,bgqkd->bghqk', qq, k_sel, preferred_element_type=jnp.float32) * scale
        s_s = s_s + lookup(rel_g, _t5_bucket(dist))
        p_s = _masked_softmax(s_s, (dist >= 0)[:, :, None])
        o_s = jnp.einsum('bghqk,bgqkd->bghqd', p_s.astype(v_sel.dtype), v_sel)
        kwb = lax.dynamic_slice_in_dim(kw, q0, Q_BLOCK + NSA_WINDOW, axis=2)
        vwb = lax.dynamic_slice_in_dim(vw, q0, Q_BLOCK + NSA_WINDOW, axis=2)
        s_w = jnp.einsum('bghqd,bgkd->bghqk', qq, kwb, preferred_element_type=jnp.float32) * scale + bias_w
        mask_w = band_w & ((q0 - NSA_WINDOW + j_w) >= 0)[None]
        p_w = _masked_softmax(s_w, mask_w)
        o_w = jnp.einsum('bghqk,bgkd->bghqd', p_w.astype(vwb.dtype), vwb)
        return gg[..., 0:1] * o_c + gg[..., 1:2] * o_s + gg[..., 2:3] * o_w

    out = lax.map(block, jnp.arange(nb))
    return out.transpose(1, 0, 4, 2, 3, 5).reshape(B, S, W_A).astype(q_a.dtype)


def _swa_sink(q_b, k_b, v_b, sinks, rel_b):
    B, S, _ = q_b.shape
    G, HPG, dh, W = SWA_GROUPS, SWA_HEADS // SWA_GROUPS, HEAD_DIM, SWA_WINDOW
    nb = S // W
    scale = HEAD_DIM ** -0.5
    q = q_b.reshape(B, nb, W, G, HPG, dh)
    k = k_b.reshape(B, nb, W, G, dh)
    v = v_b.reshape(B, nb, W, G, dh)

    def with_prev(z):
        prev = jnp.pad(z[:, :-1], ((0, 0), (1, 0), (0, 0), (0, 0), (0, 0)))
        return jnp.concatenate([prev, z], axis=2)

    kk, vv = with_prev(k), with_prev(v)
    i = jnp.arange(W)
    j = jnp.arange(2 * W)
    dist = W + i[:, None] - j[None]
    band = (dist >= 0) & (dist < W)
    first = (jnp.arange(nb)[:, None] * W - W + j[None]) >= 0
    mask = band[None] & first[:, None, :]
    bias = rel_b.reshape(G, HPG, REL_BUCKETS)[:, :, _t5_bucket(dist)]
    logits = jnp.einsum('bnqghd,bnkgd->bghnqk', q, kk, preferred_element_type=jnp.float32) * scale
    logits = jnp.where(mask, logits + bias[:, :, None], -jnp.inf)
    sink = jnp.broadcast_to(sinks.astype(jnp.float32).reshape(1, G, HPG, 1, 1, 1), logits.shape[:-1] + (1,))
    p = jax.nn.softmax(jnp.concatenate([logits, sink], axis=-1), axis=-1)[..., :-1]
    o = jnp.einsum('bghnqk,bnkgd->bnqghd', p.astype(vv.dtype), vv)
    return o.reshape(B, S, W_B)


def _fox(q_c, k_c, v_c, f_c, b_f):
    B, S, _ = q_c.shape
    H, dh = FOX_HEADS, HEAD_DIM
    nb = S // Q_BLOCK
    scale = HEAD_DIM ** -0.5
    q = q_c.reshape(B, S, H, dh).transpose(0, 2, 1, 3)
    k = k_c.reshape(B, S, H, dh).transpose(0, 2, 1, 3)
    v = v_c.reshape(B, S, H, dh).transpose(0, 2, 1, 3)
    log_f = jax.nn.log_sigmoid((f_c + b_f).astype(jnp.float32))
    c = jnp.cumsum(log_f, axis=1).transpose(0, 2, 1)
    s_idx = jnp.arange(S)

    def block(qb):
        q0 = qb * Q_BLOCK
        t = q0 + jnp.arange(Q_BLOCK)
        qq = lax.dynamic_slice_in_dim(q, q0, Q_BLOCK, axis=2)
        cq = lax.dynamic_slice_in_dim(c, q0, Q_BLOCK, axis=2)
        logits = jnp.einsum('bhqd,bhkd->bhqk', qq, k, preferred_element_type=jnp.float32) * scale
        logits = logits + cq[..., None] - c[:, :, None, :]
        logits = jnp.where(s_idx[None] <= t[:, None], logits, -jnp.inf)
        p = jax.nn.softmax(logits, axis=-1)
        return jnp.einsum('bhqk,bhkd->bhqd', p.astype(v.dtype), v)

    out = lax.map(block, jnp.arange(nb))
    return out.transpose(1, 0, 3, 2, 4).reshape(B, S, W_C)


def _token_mix(h, w_in, pe_k, pe_v, ck_w1, ck_w2, cv_w1, cv_w2, sinks, b_f, rel_bias,
               w_br_a, w_br_b, w_br_c, w_gate, b_gate, w_out):
    B, S, D = h.shape
    q_a, kv_a, g_a, q_b, k_b, v_b, q_c, k_c, v_c, f_c = _split_cols(h @ w_in)
    rel_t = rel_bias.T
    o_a = _nsa(q_a, kv_a, g_a, pe_k, pe_v, ck_w1, ck_w2, cv_w1, cv_w2, rel_t[:NSA_HEADS])
    o_b = _swa_sink(q_b, k_b, v_b, sinks, rel_t[NSA_HEADS:])
    o_c = _fox(q_c, k_c, v_c, f_c, b_f)
    g = jax.nn.sigmoid(h @ w_gate + b_gate).reshape(B, S, N_BRANCH, D)
    merged = g[:, :, 0] * (o_a @ w_br_a) + g[:, :, 1] * (o_b @ w_br_b) + g[:, :, 2] * (o_c @ w_br_c)
    return merged @ w_out


def setup_inputs(seed: int = 0) -> dict:
    key = jax.random.key(seed)
    keys = iter(jax.random.split(key, 32))
    L, D, F = DEPTH, D_MODEL, D_FF

    def nrm(shape, scale):
        return jax.random.normal(next(keys), shape, jnp.float32) * scale

    return {
        'x': nrm((BATCH, SEQ, D), 1.0),
        'rel_bias': nrm((REL_BUCKETS, NSA_HEADS + SWA_HEADS), 0.5),
        'ln1_g': 1.0 + nrm((L, D), 0.02),
        'ln1_b': nrm((L, D), 0.02),
        'ffn1_w1': nrm((L, D, 2 * F), D ** -0.5),
        'ffn1_w2': nrm((L, F, D), DN_BETA * F ** -0.5),
        'w_in': nrm((L, D, D_IN), D ** -0.5),
        'cmp_pe_k': nrm((L, CMP_BLOCK, HEAD_DIM), 0.1),
        'cmp_pe_v': nrm((L, CMP_BLOCK, HEAD_DIM), 0.1),
        'cmp_k_w1': nrm((L, CMP_BLOCK * HEAD_DIM, CMP_HIDDEN), (CMP_BLOCK * HEAD_DIM) ** -0.5),
        'cmp_k_w2': nrm((L, CMP_HIDDEN, HEAD_DIM), CMP_HIDDEN ** -0.5),
        'cmp_v_w1': nrm((L, CMP_BLOCK * HEAD_DIM, CMP_HIDDEN), (CMP_BLOCK * HEAD_DIM) ** -0.5),
        'cmp_v_w2': nrm((L, CMP_HIDDEN, HEAD_DIM), CMP_HIDDEN ** -0.5),
        'swa_sinks': nrm((L, SWA_HEADS), 1.0),
        'fox_b_f': FOX_GATE_BIAS + nrm((L, FOX_HEADS), 0.1),
        'w_br_a': nrm((L, W_A, D), W_A ** -0.5),
        'w_br_b': nrm((L, W_B, D), W_B ** -0.5),
        'w_br_c': nrm((L, W_C, D), W_C ** -0.5),
        'w_gate': nrm((L, D, N_BRANCH * D), D ** -0.5),
        'b_gate': nrm((L, N_BRANCH * D), 0.01),
        'w_out': nrm((L, D, D), DN_BETA * D ** -0.5),
        'ln2_g': 1.0 + nrm((L, D), 0.02),
        'ln2_b': nrm((L, D), 0.02),
        'ffn2_w1': nrm((L, D, 2 * F), D ** -0.5),
        'ffn2_w2': nrm((L, F, D), DN_BETA * F ** -0.5),
        'ln3_g': 1.0 + nrm((L, D), 0.02),
        'ln3_b': nrm((L, D), 0.02),
    }


def reference(x, rel_bias, ln1_g, ln1_b, ffn1_w1, ffn1_w2, w_in, cmp_pe_k, cmp_pe_v,
              cmp_k_w1, cmp_k_w2, cmp_v_w1, cmp_v_w2, swa_sinks, fox_b_f,
              w_br_a, w_br_b, w_br_c, w_gate, b_gate, w_out, ln2_g, ln2_b,
              ffn2_w1, ffn2_w2, ln3_g, ln3_b):
    for l in range(DEPTH):
        x = _layer_norm(DN_ALPHA * x + 0.5 * _swiglu(x, ffn1_w1[l], ffn1_w2[l]), ln1_g[l], ln1_b[l])
        mix = _token_mix(x, w_in[l], cmp_pe_k[l], cmp_pe_v[l], cmp_k_w1[l], cmp_k_w2[l],
                         cmp_v_w1[l], cmp_v_w2[l], swa_sinks[l], fox_b_f[l], rel_bias,
                         w_br_a[l], w_br_b[l], w_br_c[l], w_gate[l], b_gate[l], w_out[l])
        x = _layer_norm(DN_ALPHA * x + mix, ln2_g[l], ln2_b[l])
        x = _layer_norm(DN_ALPHA * x + 0.5 * _swiglu(x, ffn2_w1[l], ffn2_w2[l]), ln3_g[l], ln3_b[l])
    return x
```

```python
import functools
import math

import numpy as np
import jax
import jax.numpy as jnp
from jax import lax
from jax.experimental import pallas as pl
from jax.experimental.pallas import tpu as pltpu

F32 = jnp.float32
BF16 = jnp.bfloat16

D_MODEL = 1024
DEPTH = 4
HEAD_DIM = 64
NSA_HEADS = 8
NSA_GROUPS = 2
HPG = NSA_HEADS // NSA_GROUPS
CMP_BLOCK = 32
CMP_STRIDE = 16
CMP_HIDDEN = 256
SEL_BLOCK = 64
SEL_TOPK = 8
NSA_WINDOW = 512
SWA_HEADS = 8
SWA_GROUPS = 2
SWA_WINDOW = 128
FOX_HEADS = 8
Q_BLOCK = 128
REL_BUCKETS = 32
REL_MAX_DIST = 128
D_FF = 2816
LN_EPS = 1e-5
DN_ALPHA = (2 * DEPTH) ** 0.25
SCALE = HEAD_DIM ** -0.5

NEG = -1e30
NEG_TEST = -1e29
LANES = 128
VMEM_LIMIT = 56 * 1024 * 1024

ROW_QA, ROW_QB, ROW_QC = 0, 512, 1024
ROW_VSLC, ROW_VWIN, ROW_VB, ROW_VC = 1536, 1664, 1792, 1920
ROWS_FEAT = 2432
CB_KC, CB_KSLC, CB_KWIN, CB_KB = 0, 8, 10, 12
COLS_K = 14 * LANES
N_GA = 3 * NSA_HEADS
COL_FC = N_GA
ROWS_SMALL = 32
LANE_C = HEAD_DIM


def _params(sem, vmem=VMEM_LIMIT):
    return pltpu.CompilerParams(dimension_semantics=sem, vmem_limit_bytes=vmem)


def _dot(a, b):
    return jnp.dot(a, b, preferred_element_type=F32)


def _dot_nt(a, b):
    return lax.dot_general(a, b, (((1,), (1,)), ((), ())), preferred_element_type=F32)


def _dot_tn(a, b):
    return lax.dot_general(a, b, (((0,), (0,)), ((), ())), preferred_element_type=F32)


def _layer_norm(y, g, b):
    mu = jnp.mean(y, axis=-1, keepdims=True)
    yc = y - mu
    var = jnp.mean(yc * yc, axis=-1, keepdims=True)
    return yc * lax.rsqrt(var + LN_EPS) * g + b


def _bucket_np(n):
    n = np.maximum(n, 0)
    exact = REL_BUCKETS // 2
    ratio = np.log(np.maximum(n, 1).astype(np.float64) / exact) / math.log(REL_MAX_DIST / exact)
    large = exact + (ratio * (REL_BUCKETS - exact)).astype(np.int64)
    return np.where(n < exact, n, np.minimum(large, REL_BUCKETS - 1)).astype(np.int32)


def _static_tables(seq):
    nch = seq // CMP_STRIDE
    ns = seq // SEL_BLOCK
    r = np.arange(Q_BLOCK)[None, :]
    u = np.arange(2 * nch)[:, None]
    n_c = r - (CMP_BLOCK - 1) - CMP_STRIDE * (u - nch)
    bk_c = np.where(n_c >= 0, _bucket_np(n_c), -1).astype(np.int32)
    c = np.arange(Q_BLOCK)[:, None]
    d_prev = Q_BLOCK + r - c
    d_diag = r - c
    bk_prev = _bucket_np(d_prev)
    bk_diag = np.where(d_diag >= 0, _bucket_np(d_diag), -1)
    bk_near = np.concatenate([bk_prev, bk_diag], axis=0).astype(np.int32)
    bk_swa = np.concatenate([np.where(d_prev < SWA_WINDOW, bk_prev, -1), bk_diag], axis=0).astype(np.int32)
    band0 = np.where(c > r, 0.0, NEG).astype(np.float32)
    i = np.arange(nch)[None, :]
    j = np.arange(64)[:, None]
    ov = (CMP_STRIDE * i < SEL_BLOCK * j + SEL_BLOCK) & (CMP_STRIDE * i + CMP_BLOCK - 1 >= SEL_BLOCK * j)
    ov &= (i < nch - 1) & (j < ns)
    s = np.arange(seq)[:, None]
    lane = np.arange(LANES)[None, :]
    onehot = (lane == HEAD_DIM + s // SEL_BLOCK).astype(np.float32)
    return dict(bk_c=bk_c, bk_near=bk_near, bk_swa=bk_swa, band0=band0,
                ovT=ov.astype(np.float32), onehot=onehot)


def _tables_kernel(rel_ref, bkc_ref, bkn_ref, bks_ref, u_ref, bn_ref, bs_ref):
    h = pl.program_id(0)

    def build(bk, col, delta):
        base = rel_ref[REL_BUCKETS - 1, col] if delta else 0.0
        acc = jnp.zeros(bk.shape, F32)
        for b in range(REL_BUCKETS):
            acc = jnp.where(bk == b, rel_ref[b, col] - base, acc)
        return jnp.where(bk < 0, NEG, acc)

    u_ref[0] = build(bkc_ref[...], h, True)
    bn_ref[0] = build(bkn_ref[...], h, True)
    bs_ref[0] = build(bks_ref[...], NSA_HEADS + h, False)


def _bias_tables(rel_bias, tabs):
    nch2 = tabs["bk_c"].shape[0]
    full = lambda shp: pl.BlockSpec(shp, lambda h: (0,) * len(shp))
    return pl.pallas_call(
        _tables_kernel,
        name="bias_tables",
        grid=(NSA_HEADS,),
        in_specs=[pl.BlockSpec(memory_space=pltpu.SMEM),
                  full((nch2, LANES)), full((2 * Q_BLOCK, LANES)), full((2 * Q_BLOCK, LANES))],
        out_specs=[pl.BlockSpec((1, nch2, LANES), lambda h: (h, 0, 0)),
                   pl.BlockSpec((1, 2 * Q_BLOCK, LANES), lambda h: (h, 0, 0)),
                   pl.BlockSpec((1, 2 * Q_BLOCK, LANES), lambda h: (h, 0, 0))],
        out_shape=[jax.ShapeDtypeStruct((NSA_HEADS, nch2, LANES), F32),
                   jax.ShapeDtypeStruct((NSA_HEADS, 2 * Q_BLOCK, LANES), F32),
                   jax.ShapeDtypeStruct((SWA_HEADS, 2 * Q_BLOCK, LANES), F32)],
        compiler_params=_params(("arbitrary",)),
    )(rel_bias, jnp.asarray(tabs["bk_c"]), jnp.asarray(tabs["bk_near"]), jnp.asarray(tabs["bk_swa"]))


FFN_TM = 1024
FFN_TF = 256


def _ffn_ln_kernel(x_ref, w1g_ref, w1u_ref, w2_ref, g_ref, b_ref, o_ref, ob_ref, acc_ref, xb_ref):
    j = pl.program_id(1)

    @pl.when(j == 0)
    def _init():
        acc_ref[...] = jnp.zeros_like(acc_ref)
        xb_ref[...] = x_ref[...].astype(BF16)

    xb = xb_ref[...]
    gt = _dot(xb, w1g_ref[...])
    up = _dot(xb, w1u_ref[...])
    act = (gt * jax.nn.sigmoid(gt) * up).astype(BF16)
    acc_ref[...] += _dot(act, w2_ref[...])

    @pl.when(j == pl.num_programs(1) - 1)
    def _finish():
        y = DN_ALPHA * x_ref[...] + 0.5 * acc_ref[...]
        out = _layer_norm(y, g_ref[...], b_ref[...])
        o_ref[...] = out
        ob_ref[...] = out.astype(BF16)


def _ffn_ln(x, w1, w2, g, b):
    n, d = x.shape
    f = w2.shape[0]
    tm = min(FFN_TM, n)
    nf = f // FFN_TF
    return pl.pallas_call(
        _ffn_ln_kernel,
        name="ffn_ln",
        grid=(n // tm, nf),
        in_specs=[pl.BlockSpec((tm, d), lambda i, j: (i, 0)),
                  pl.BlockSpec((d, FFN_TF), lambda i, j: (0, j)),
                  pl.BlockSpec((d, FFN_TF), lambda i, j: (0, j + nf)),
                  pl.BlockSpec((FFN_TF, d), lambda i, j: (j, 0)),
                  pl.BlockSpec((1, d), lambda i, j: (0, 0)),
                  pl.BlockSpec((1, d), lambda i, j: (0, 0))],
        out_specs=[pl.BlockSpec((tm, d), lambda i, j: (i, 0)),
                   pl.BlockSpec((tm, d), lambda i, j: (i, 0))],
        out_shape=[jax.ShapeDtypeStruct((n, d), F32), jax.ShapeDtypeStruct((n, d), BF16)],
        scratch_shapes=[pltpu.VMEM((tm, d), F32), pltpu.VMEM((tm, d), BF16)],
        compiler_params=_params(("arbitrary", "arbitrary")),
    )(x, w1, w1, w2, g, b)


PROJ_TM = 512


def _proj_kernel(xb_ref, wft_ref, wk_ref, wc_ref, ws_ref, wst_ref, zt_ref, zk_ref, zc_ref, fz_ref, gt_ref):
    xb = xb_ref[...]
    zt_ref[...] = _dot_nt(wft_ref[...], xb).astype(BF16)
    zk_ref[...] = _dot(xb, wk_ref[...]).astype(BF16)
    zc_ref[...] = _dot(xb, wc_ref[...]).astype(BF16)
    fz_ref[...] = _dot(xb, ws_ref[...])
    gt_ref[...] = _dot_nt(wst_ref[...], xb)


def _project(xb, wft, wk, wc, ws, wst):
    n, d = xb.shape
    tm = min(PROJ_TM, n)
    res = lambda shp: pl.BlockSpec(shp, lambda i: (0, 0))
    return pl.pallas_call(
        _proj_kernel,
        name="proj_in",
        grid=(n // tm,),
        in_specs=[pl.BlockSpec((tm, d), lambda i: (i, 0)),
                  res(wft.shape), res(wk.shape), res(wc.shape), res(ws.shape), res(wst.shape)],
        out_specs=[pl.BlockSpec((ROWS_FEAT, tm), lambda i: (0, i)),
                   pl.BlockSpec((tm, COLS_K), lambda i: (i, 0)),
                   pl.BlockSpec((tm, 4 * HEAD_DIM), lambda i: (i, 0)),
                   pl.BlockSpec((tm, LANES), lambda i: (i, 0)),
                   pl.BlockSpec((ROWS_SMALL, tm), lambda i: (0, i))],
        out_shape=[jax.ShapeDtypeStruct((ROWS_FEAT, n), BF16),
                   jax.ShapeDtypeStruct((n, COLS_K), BF16),
                   jax.ShapeDtypeStruct((n, 4 * HEAD_DIM), BF16),
                   jax.ShapeDtypeStruct((n, LANES), F32),
                   jax.ShapeDtypeStruct((ROWS_SMALL, n), F32)],
        compiler_params=_params(("arbitrary",)),
    )(xb, wft, wk, wc, ws, wst)


def _gelu_tanh(x):
    return 0.5 * x * (1.0 + jnp.tanh(math.sqrt(2.0 / math.pi) * (x + 0.044715 * (x * x * x))))


def _compress_kernel(ck_ref, cv_ref, pek_ref, pev_ref, wk1_ref, wk2_ref, wv1_ref, wv2t_ref, kc_ref, vct_ref):
    half = CMP_STRIDE * HEAD_DIM
    nch = ck_ref.shape[2]

    def hidden(c_ref, pe_ref, w1_ref):
        c = c_ref[0, 0].astype(F32)
        top = (c + pe_ref[:, :half]).astype(BF16)
        bot = (c + pe_ref[:, half:]).astype(BF16)
        a = _dot(top, w1_ref[:half, :])
        bm = _dot(bot, w1_ref[half:, :])
        h = a + pltpu.roll(bm, nch - 1, axis=0)
        return _gelu_tanh(h).astype(BF16)

    kc = _dot(hidden(ck_ref, pek_ref, wk1_ref), wk2_ref[...])
    kc_ref[0, 0] = jnp.concatenate([kc, jnp.zeros_like(kc)], axis=1).astype(BF16)
    vct_ref[0, 0] = _dot_nt(wv2t_ref[...], hidden(cv_ref, pev_ref, wv1_ref)).astype(BF16)


def _compress(zc4, pek, pev, wk1, wk2, wv1, wv2t):
    bsz, _, nch, wide = zc4.shape
    res = lambda shp: pl.BlockSpec(shp, lambda b, g: (0, 0))
    return pl.pallas_call(
        _compress_kernel,
        name="nsa_compress",
        grid=(bsz, NSA_GROUPS),
        in_specs=[pl.BlockSpec((1, 1, nch, wide), lambda b, g: (b, g, 0, 0)),
                  pl.BlockSpec((1, 1, nch, wide), lambda b, g: (b, NSA_GROUPS + g, 0, 0)),
                  res(pek.shape), res(pev.shape), res(wk1.shape), res(wk2.shape), res(wv1.shape), res(wv2t.shape)],
        out_specs=[pl.BlockSpec((1, 1, nch, LANES), lambda b, g: (b, g, 0, 0)),
                   pl.BlockSpec((1, 1, HEAD_DIM, nch), lambda b, g: (b, g, 0, 0))],
        out_shape=[jax.ShapeDtypeStruct((bsz, NSA_GROUPS, nch, LANES), BF16),
                   jax.ShapeDtypeStruct((bsz, NSA_GROUPS, HEAD_DIM, nch), BF16)],
        compiler_params=_params(("arbitrary", "arbitrary")),
    )(zc4, zc4, pek, pev, wk1, wk2, wv1, wv2t)


SEL_TK = 512


def _softmax_update(s, m, l, acc, vt):
    mn = jnp.maximum(m, jnp.max(s, axis=0, keepdims=True))
    alpha = jnp.exp(m - mn)
    p = jnp.exp(s - mn)
    l = alpha * l + jnp.sum(p, axis=0, keepdims=True)
    acc = alpha * acc + _dot(vt, p.astype(BF16))
    return mn, l, acc


def _nsa_kernel(q_ref, gate_ref, kc_ref, vct_ref, ks_ref, vst_ref, kw_ref, vwt_ref, oh_ref, ovt_ref,
                u_ref, bn_ref, band_ref, o_ref):
    g = pl.program_id(1)
    qb = pl.program_id(2)
    nch = kc_ref.shape[2]
    tq = Q_BLOCK
    wide = HPG * tq

    def lanes4(fn):
        return jnp.concatenate([fn(hp) for hp in range(HPG)], axis=1)

    qs = [q_ref[hp * HEAD_DIM:(hp + 1) * HEAD_DIM, :] * SCALE for hp in range(HPG)]
    zeros_q = jnp.zeros((HEAD_DIM, tq), BF16)
    q_pad = lanes4(lambda hp: jnp.concatenate([qs[hp], zeros_q], axis=0))

    off = pl.multiple_of(nch - 8 * qb, 8)
    s_c = _dot(kc_ref[0, 0], q_pad) + lanes4(lambda hp: u_ref[hp, pl.ds(off, nch), :])
    m_c = jnp.max(s_c, axis=0, keepdims=True)
    p_c = jnp.where(s_c > NEG_TEST, jnp.exp(s_c - m_c), 0.0)
    l_c = jnp.sum(p_c, axis=0, keepdims=True)
    p_c = p_c * (1.0 / jnp.maximum(l_c, 1e-30))
    o_c = _dot(vct_ref[0, 0], p_c.astype(BF16))

    p_sum = p_c[:, 0:tq]
    for hp in range(1, HPG):
        p_sum = p_sum + p_c[:, hp * tq:(hp + 1) * tq]
    imp = jnp.dot(ovt_ref[...], p_sum, preferred_element_type=F32, precision=lax.Precision.HIGHEST)
    nsp = imp.shape[0]
    blk = lax.broadcasted_iota(jnp.int32, (nsp, tq), 0).astype(F32)
    r_lane = lax.broadcasted_iota(jnp.int32, (nsp, tq), 1)
    cur = (2 * qb).astype(F32) + jnp.where(r_lane >= SEL_BLOCK, 1.0, 0.0)
    forced = (blk == 0.0) | (blk == cur) | (blk == cur - 1.0)
    valid = blk <= cur
    score = jnp.where(forced, jnp.inf, jnp.where(valid, imp, -jnp.inf))
    sel = jnp.zeros((nsp, tq), jnp.bool_)
    for _ in range(SEL_TOPK):
        mx = jnp.max(score, axis=0, keepdims=True)
        first = jnp.min(jnp.where(score == mx, blk, 1e9), axis=0, keepdims=True)
        hit = blk == first
        sel = sel | hit
        score = jnp.where(hit, -jnp.inf, score)
    sel = sel & valid
    selb = jnp.where(sel, 0.0, NEG)
    far_lim = (2 * qb - 2).astype(F32)
    selb_far = jnp.where(blk < far_lim, selb, NEG).astype(BF16)
    selb = selb.astype(BF16)
    q_near = lanes4(lambda hp: jnp.concatenate([qs[hp], selb], axis=0))
    q_far = lanes4(lambda hp: jnp.concatenate([qs[hp], selb_far], axis=0))

    bn_prev = lanes4(lambda hp: bn_ref[hp, 0:tq, :])
    bn_diag = lanes4(lambda hp: bn_ref[hp, tq:2 * tq, :])

    def far_body(it, carry):
        k0 = pl.multiple_of(it * SEL_TK, SEL_TK)
        ka = ks_ref[pl.ds(k0, SEL_TK), :] + oh_ref[pl.ds(k0, SEL_TK), :]
        s = _dot(ka, q_far)
        return _softmax_update(s, *carry, vst_ref[:, pl.ds(k0, SEL_TK)])

    n_far = (jnp.maximum(qb - 1, 0) + (SEL_TK // tq - 1)) // (SEL_TK // tq)
    init = (jnp.full((1, wide), -jnp.inf, F32), jnp.zeros((1, wide), F32), jnp.zeros((HEAD_DIM, wide), F32))
    m_s, l_s, acc_s = lax.fori_loop(0, n_far, far_body, init)

    p0 = pl.multiple_of(jnp.maximum(qb - 1, 0) * tq, tq)
    d0 = pl.multiple_of(qb * tq, tq)
    kp = ks_ref[pl.ds(p0, tq), :] + oh_ref[pl.ds(p0, tq), :]
    kd = ks_ref[pl.ds(d0, tq), :] + oh_ref[pl.ds(d0, tq), :]
    s_p = jnp.where(qb > 0, _dot(kp, q_near) + bn_prev, NEG)
    s_d = _dot(kd, q_near) + bn_diag
    s_n = jnp.concatenate([s_p, s_d], axis=0)
    vt_n = jnp.concatenate([vst_ref[:, pl.ds(p0, tq)], vst_ref[:, pl.ds(d0, tq)]], axis=1)
    m_s, l_s, acc_s = _softmax_update(s_n, m_s, l_s, acc_s, vt_n)
    o_s = acc_s * (1.0 / l_s)

    n_win = NSA_WINDOW // tq
    band4 = lanes4(lambda hp: band_ref[...])
    s_tiles, v_tiles = [], []
    for a in range(n_win + 1):
        ti = qb - n_win + a
        t0 = pl.multiple_of(jnp.maximum(ti, 0) * tq, tq)
        s_a = _dot(kw_ref[pl.ds(t0, tq), :], q_pad)
        if a == 0:
            s_a = s_a + band4
        if a == n_win - 1:
            s_a = s_a + bn_prev
        if a == n_win:
            s_a = s_a + bn_diag
        else:
            s_a = jnp.where(ti >= 0, s_a, NEG)
        s_tiles.append(s_a)
        v_tiles.append(vwt_ref[:, pl.ds(t0, tq)])
    s_w = jnp.concatenate(s_tiles, axis=0)
    m_w = jnp.max(s_w, axis=0, keepdims=True)
    p_w = jnp.exp(s_w - m_w)
    l_w = jnp.sum(p_w, axis=0, keepdims=True)
    o_w = _dot(jnp.concatenate(v_tiles, axis=1), p_w.astype(BF16)) * (1.0 / l_w)

    for hp in range(HPG):
        row = g * (HPG * 3) + hp * 3
        gates = jax.nn.sigmoid(gate_ref[pl.ds(row, 3), :])
        sl = slice(hp * tq, (hp + 1) * tq)
        out = gates[0:1] * o_c[:, sl] + gates[1:2] * o_s[:, sl] + gates[2:3] * o_w[:, sl]
        o_ref[hp * HEAD_DIM:(hp + 1) * HEAD_DIM, :] = out.astype(BF16)


def _nsa(zt, zk, gt, kc, vct, onehot, ovt, u_tab, bn_tab, band0, bsz, seq):
    nq = seq // Q_BLOCK
    nch = seq // CMP_STRIDE
    gw = HPG * HEAD_DIM
    n = bsz * seq
    const2 = lambda shp: pl.BlockSpec(shp, lambda b, g, q: (0, 0))
    return pl.pallas_call(
        _nsa_kernel,
        name="nsa_attn",
        grid=(bsz, NSA_GROUPS, nq),
        in_specs=[pl.BlockSpec((gw, Q_BLOCK), lambda b, g, q: (ROW_QA // gw + g, b * nq + q)),
                  pl.BlockSpec((ROWS_SMALL, Q_BLOCK), lambda b, g, q: (0, b * nq + q)),
                  pl.BlockSpec((1, 1, nch, LANES), lambda b, g, q: (b, g, 0, 0)),
                  pl.BlockSpec((1, 1, HEAD_DIM, nch), lambda b, g, q: (b, g, 0, 0)),
                  pl.BlockSpec((seq, LANES), lambda b, g, q: (b, CB_KSLC + g)),
                  pl.BlockSpec((HEAD_DIM, seq), lambda b, g, q: (ROW_VSLC // HEAD_DIM + g, b)),
                  pl.BlockSpec((seq, LANES), lambda b, g, q: (b, CB_KWIN + g)),
                  pl.BlockSpec((HEAD_DIM, seq), lambda b, g, q: (ROW_VWIN // HEAD_DIM + g, b)),
                  const2(onehot.shape), const2(ovt.shape),
                  pl.BlockSpec((HPG,) + u_tab.shape[1:], lambda b, g, q: (g, 0, 0)),
                  pl.BlockSpec((HPG,) + bn_tab.shape[1:], lambda b, g, q: (g, 0, 0)),
                  const2(band0.shape)],
        out_specs=pl.BlockSpec((gw, Q_BLOCK), lambda b, g, q: (g, b * nq + q)),
        out_shape=jax.ShapeDtypeStruct((NSA_HEADS * HEAD_DIM, n), BF16),
        compiler_params=_params(("arbitrary", "arbitrary", "arbitrary")),
    )(zt, gt, kc, vct, zk, zt, zk, zt, onehot, ovt, u_tab, bn_tab, band0)


SWA_SUB = 4


def _swa_kernel(sink_ref, q_ref, kcur_ref, kprev_ref, vcur_ref, vprev_ref, bs_ref, o_ref):
    g = pl.program_id(1)
    t = pl.program_id(2)
    tq = SWA_WINDOW
    zeros_q = jnp.zeros((HEAD_DIM, tq), BF16)

    def lanes4(fn):
        return jnp.concatenate([fn(hp) for hp in range(HPG)], axis=1)

    bs_prev = lanes4(lambda hp: bs_ref[hp, 0:tq, :])
    bs_diag = lanes4(lambda hp: bs_ref[hp, tq:2 * tq, :])
    sink = lanes4(lambda hp: jnp.full((1, tq), sink_ref[g * HPG + hp], F32))
    for sub in range(SWA_SUB):
        cs = slice(sub * tq, (sub + 1) * tq)
        q_pad = lanes4(lambda hp: jnp.concatenate(
            [q_ref[hp * HEAD_DIM:(hp + 1) * HEAD_DIM, cs] * SCALE, zeros_q], axis=0))
        if sub == 0:
            kp, vp = kprev_ref[...], vprev_ref[...]
        else:
            ps = slice((sub - 1) * tq, sub * tq)
            kp, vp = kcur_ref[ps, :], vcur_ref[:, ps]
        s_p = _dot(kp, q_pad) + bs_prev
        if sub == 0:
            s_p = jnp.where(t > 0, s_p, NEG)
        s_d = _dot(kcur_ref[cs, :], q_pad) + bs_diag
        s = jnp.concatenate([s_p, s_d], axis=0)
        m = jnp.maximum(jnp.max(s, axis=0, keepdims=True), sink)
        p = jnp.exp(s - m)
        den = jnp.sum(p, axis=0, keepdims=True) + jnp.exp(sink - m)
        o = _dot(jnp.concatenate([vp, vcur_ref[:, cs]], axis=1), p.astype(BF16)) * (1.0 / den)
        for hp in range(HPG):
            o_ref[hp * HEAD_DIM:(hp + 1) * HEAD_DIM, cs] = o[:, hp * tq:(hp + 1) * tq].astype(BF16)


def _swa(zt, zk, sinks, bs_tab, bsz, seq):
    tq = SWA_WINDOW
    nq = seq // tq
    nt = nq // SWA_SUB
    wide = SWA_SUB * tq
    gw = HPG * HEAD_DIM
    n = bsz * seq
    prev = lambda b, t: b * nq + jnp.maximum(t * SWA_SUB - 1, 0)
    return pl.pallas_call(
        _swa_kernel,
        name="swa_attn",
        grid=(bsz, SWA_GROUPS, nt),
        in_specs=[pl.BlockSpec(memory_space=pltpu.SMEM),
                  pl.BlockSpec((gw, wide), lambda b, g, t: (ROW_QB // gw + g, b * nt + t)),
                  pl.BlockSpec((wide, LANES), lambda b, g, t: (b * nt + t, CB_KB + g)),
                  pl.BlockSpec((tq, LANES), lambda b, g, t: (prev(b, t), CB_KB + g)),
                  pl.BlockSpec((HEAD_DIM, wide), lambda b, g, t: (ROW_VB // HEAD_DIM + g, b * nt + t)),
                  pl.BlockSpec((HEAD_DIM, tq), lambda b, g, t: (ROW_VB // HEAD_DIM + g, prev(b, t))),
                  pl.BlockSpec((HPG,) + bs_tab.shape[1:], lambda b, g, t: (g, 0, 0))],
        out_specs=pl.BlockSpec((gw, wide), lambda b, g, t: (g, b * nt + t)),
        out_shape=jax.ShapeDtypeStruct((SWA_HEADS * HEAD_DIM, n), BF16),
        compiler_params=_params(("arbitrary", "arbitrary", "arbitrary")),
    )(sinks, zt, zk, zk, zt, zt, bs_tab)


def _fox_keys_kernel(fz_ref, bf_ref, k_ref, o_ref, c_ref):
    seq = fz_ref.shape[0]
    ch = LANES
    logf = jax.nn.log_sigmoid(fz_ref[...] + bf_ref[...])
    tri = (lax.broadcasted_iota(jnp.int32, (ch, ch), 0) >= lax.broadcasted_iota(jnp.int32, (ch, ch), 1)).astype(F32)
    carry = jnp.zeros((1, LANES), F32)
    for i in range(seq // ch):
        c = jnp.dot(tri, logf[i * ch:(i + 1) * ch, :], preferred_element_type=F32,
                    precision=lax.Precision.HIGHEST) + carry
        c_ref[i * ch:(i + 1) * ch, :] = c
        carry = c[ch - 1:ch, :]
    neg = -c_ref[...]
    hi = neg.astype(BF16)
    r1 = neg - hi.astype(F32)
    mid = r1.astype(BF16)
    lo = (r1 - mid.astype(F32)).astype(BF16)
    parts = jnp.concatenate([hi, mid, lo], axis=1)
    row = lax.broadcasted_iota(jnp.int32, (3 * LANES, LANES), 0)
    col = lax.broadcasted_iota(jnp.int32, (3 * LANES, LANES), 1)
    lane = lax.broadcasted_iota(jnp.int32, (seq, LANES), 1)
    for h in range(FOX_HEADS):
        src = COL_FC + h
        place = ((row == src) & (col == LANE_C)) | ((row == LANES + src) & (col == LANE_C + 1)) \
            | ((row == 2 * LANES + src) & (col == LANE_C + 2))
        bias = _dot(parts, place.astype(BF16)).astype(BF16)
        hs = slice(h * LANES, (h + 1) * LANES)
        o_ref[:, hs] = jnp.where(lane < HEAD_DIM, k_ref[:, hs], bias)


def _fox_keys(fz, bf_row, zk, bsz, seq):
    wide = FOX_HEADS * LANES
    return pl.pallas_call(
        _fox_keys_kernel,
        name="fox_keys",
        grid=(bsz,),
        in_specs=[pl.BlockSpec((seq, LANES), lambda b: (b, 0)),
                  pl.BlockSpec((1, LANES), lambda b: (0, 0)),
                  pl.BlockSpec((seq, wide), lambda b: (b, CB_KC))],
        out_specs=pl.BlockSpec((seq, wide), lambda b: (b, 0)),
        out_shape=jax.ShapeDtypeStruct((bsz * seq, wide), BF16),
        scratch_shapes=[pltpu.VMEM((seq, LANES), F32)],
        compiler_params=_params(("arbitrary",)),
    )(fz, bf_row, zk)


FOX_T = 512


def _fox_kernel(q_ref, k_ref, vt_ref, o_ref):
    qi = pl.program_id(2)
    t = q_ref.shape[1]
    ones_rows = (lax.broadcasted_iota(jnp.int32, (HEAD_DIM, t), 0) < 3).astype(BF16)
    q_aug = jnp.concatenate([q_ref[...] * SCALE, ones_rows], axis=0)

    def body(i, carry):
        k0 = pl.multiple_of(i * t, t)
        s = _dot(k_ref[pl.ds(k0, t), :], q_aug)
        return _softmax_update(s, *carry, vt_ref[:, pl.ds(k0, t)])

    init = (jnp.full((1, t), -jnp.inf, F32), jnp.zeros((1, t), F32), jnp.zeros((HEAD_DIM, t), F32))
    carry = lax.fori_loop(0, qi, body, init)
    d0 = pl.multiple_of(qi * t, t)
    s = _dot(k_ref[pl.ds(d0, t), :], q_aug)
    future = lax.broadcasted_iota(jnp.int32, (t, t), 0) > lax.broadcasted_iota(jnp.int32, (t, t), 1)
    s = jnp.where(future, NEG, s)
    _, l, acc = _softmax_update(s, *carry, vt_ref[:, pl.ds(d0, t)])
    o_ref[...] = (acc * (1.0 / l)).astype(BF16)


def _fox(zt, kaug, bsz, seq):
    t = min(FOX_T, seq)
    nq = seq // t
    n = bsz * seq
    return pl.pallas_call(
        _fox_kernel,
        name="fox_attn",
        grid=(bsz, FOX_HEADS, nq),
        in_specs=[pl.BlockSpec((HEAD_DIM, t), lambda b, h, q: (ROW_QC // HEAD_DIM + h, b * nq + q)),
                  pl.BlockSpec((seq, LANES), lambda b, h, q: (b, h)),
                  pl.BlockSpec((HEAD_DIM, seq), lambda b, h, q: (ROW_VC // HEAD_DIM + h, b))],
        out_specs=pl.BlockSpec((HEAD_DIM, t), lambda b, h, q: (h, b * nq + q)),
        out_shape=jax.ShapeDtypeStruct((FOX_HEADS * HEAD_DIM, n), BF16),
        compiler_params=_params(("arbitrary", "arbitrary", "arbitrary")),
    )(zt, kaug, zt)


MERGE_TM = 512


def _merge_ln_kernel(x_ref, xb_ref, oa_ref, ob_ref, oc_ref, wg_ref, bg_ref, wbr_ref, wo_ref, g_ref, b_ref,
                     o_ref, obf_ref):
    d = x_ref.shape[1]
    xb = xb_ref[...]
    merged = None
    for i, br_ref in enumerate((oa_ref, ob_ref, oc_ref)):
        gate = jax.nn.sigmoid(_dot(xb, wg_ref[:, i * d:(i + 1) * d]) + bg_ref[:, i * d:(i + 1) * d])
        term = gate * _dot_tn(br_ref[...], wbr_ref[i])
        merged = term if merged is None else merged + term
    mix = _dot(merged.astype(BF16), wo_ref[...])
    out = _layer_norm(DN_ALPHA * x_ref[...] + mix, g_ref[...], b_ref[...])
    o_ref[...] = out
    obf_ref[...] = out.astype(BF16)


def _merge_ln(x, xb, ota, otb, otc, wg, bg, wbr, wo, g, b):
    n, d = x.shape
    tm = min(MERGE_TM, n)
    w = ota.shape[0]
    tok = pl.BlockSpec((tm, d), lambda i: (i, 0))
    feat = pl.BlockSpec((w, tm), lambda i: (0, i))
    res2 = lambda shp: pl.BlockSpec(shp, lambda i: (0, 0))
    return pl.pallas_call(
        _merge_ln_kernel,
        name="merge_ln",
        grid=(n // tm,),
        in_specs=[tok, tok, feat, feat, feat, res2(wg.shape), res2(bg.shape),
                  pl.BlockSpec(wbr.shape, lambda i: (0, 0, 0)), res2(wo.shape), res2(g.shape), res2(b.shape)],
        out_specs=[tok, tok],
        out_shape=[jax.ShapeDtypeStruct((n, d), F32), jax.ShapeDtypeStruct((n, d), BF16)],
        compiler_params=_params(("arbitrary",)),
    )(x, xb, ota, otb, otc, wg, bg, wbr, wo, g, b)


def _split_w_in(w_in):
    sizes = [NSA_HEADS * HEAD_DIM, 6 * NSA_GROUPS * HEAD_DIM, N_GA, SWA_HEADS * HEAD_DIM,
             SWA_GROUPS * HEAD_DIM, SWA_GROUPS * HEAD_DIM, FOX_HEADS * HEAD_DIM, FOX_HEADS * HEAD_DIM,
             FOX_HEADS * HEAD_DIM, FOX_HEADS]
    q_a, kv_a, g_a, q_b, k_b, v_b, q_c, k_c, v_c, f_c = jnp.split(w_in, np.cumsum(sizes)[:-1].tolist(), axis=-1)
    k_cmp, v_cmp, k_slc, v_slc, k_win, v_win = jnp.split(kv_a, 6, axis=-1)
    depth, d = w_in.shape[0], w_in.shape[1]

    def pad_heads(w):
        nh = w.shape[-1] // HEAD_DIM
        w = w.reshape(depth, d, nh, HEAD_DIM)
        return jnp.concatenate([w, jnp.zeros_like(w)], axis=-1).reshape(depth, d, nh * LANES)

    w_feat_t = jnp.swapaxes(jnp.concatenate([q_a, q_b, q_c, v_slc, v_win, v_b, v_c], axis=-1), 1, 2)
    w_k = jnp.concatenate([pad_heads(k_c), pad_heads(k_slc), pad_heads(k_win), pad_heads(k_b)], axis=-1)
    w_cmp = jnp.concatenate([k_cmp, v_cmp], axis=-1)
    small = jnp.concatenate([g_a, f_c], axis=-1)
    w_small = jnp.concatenate([small, jnp.zeros((depth, d, LANES - ROWS_SMALL), small.dtype)], axis=-1)
    w_small_t = jnp.swapaxes(small, 1, 2)
    return w_feat_t, w_k, w_cmp, w_small, w_small_t


def kernel(x, rel_bias, ln1_g, ln1_b, ffn1_w1, ffn1_w2, w_in, cmp_pe_k, cmp_pe_v, cmp_k_w1, cmp_k_w2, cmp_v_w1,
           cmp_v_w2, swa_sinks, fox_b_f, w_br_a, w_br_b, w_br_c, w_gate, b_gate, w_out, ln2_g, ln2_b, ffn2_w1,
           ffn2_w2, ln3_g, ln3_b):
    bsz, seq, d = x.shape
    n = bsz * seq
    depth = w_in.shape[0]
    nch = seq // CMP_STRIDE
    tabs = _static_tables(seq)
    onehot = jnp.asarray(tabs["onehot"], BF16)
    ovt = jnp.asarray(tabs["ovT"])
    band0 = jnp.asarray(tabs["band0"])
    u_tab, bn_tab, bs_tab = _bias_tables(rel_bias, tabs)

    bf = lambda w: w.astype(BF16)
    w_feat_t, w_k, w_cmp, w_small, w_small_t = [bf(w) for w in _split_w_in(w_in)]
    f1w1, f1w2, f2w1, f2w2 = bf(ffn1_w1), bf(ffn1_w2), bf(ffn2_w1), bf(ffn2_w2)
    ck1, ck2, cv1 = bf(cmp_k_w1), bf(cmp_k_w2), bf(cmp_v_w1)
    cv2t = bf(jnp.swapaxes(cmp_v_w2, 1, 2))
    pek = cmp_pe_k.reshape(depth, 1, CMP_BLOCK * HEAD_DIM)
    pev = cmp_pe_v.reshape(depth, 1, CMP_BLOCK * HEAD_DIM)
    wg, wo = bf(w_gate), bf(w_out)
    wbr = bf(jnp.stack([w_br_a, w_br_b, w_br_c], axis=1))
    bf_rows = jnp.zeros((depth, 1, LANES), F32).at[:, 0, COL_FC:COL_FC + FOX_HEADS].set(fox_b_f)
    row = lambda v: v.reshape(1, -1)

    h = x.reshape(n, d)
    for l in range(depth):
        h, hb = _ffn_ln(h, f1w1[l], f1w2[l], row(ln1_g[l]), row(ln1_b[l]))
        zt, zk, zc, fz, gt = _project(hb, w_feat_t[l], w_k[l], w_cmp[l], w_small[l], w_small_t[l])
        zc4 = zc.reshape(bsz, nch, CMP_STRIDE, 2 * NSA_GROUPS, HEAD_DIM).transpose(0, 3, 1, 2, 4)
        zc4 = zc4.reshape(bsz, 2 * NSA_GROUPS, nch, CMP_STRIDE * HEAD_DIM)
        kc, vct = _compress(zc4, pek[l], pev[l], ck1[l], ck2[l], cv1[l], cv2t[l])
        ot_a = _nsa(zt, zk, gt, kc, vct, onehot, ovt, u_tab, bn_tab, band0, bsz, seq)
        ot_b = _swa(zt, zk, swa_sinks[l], bs_tab, bsz, seq)
        kaug = _fox_keys(fz, bf_rows[l], zk, bsz, seq)
        ot_c = _fox(zt, kaug, bsz, seq)
        h, hb = _merge_ln(h, hb, ot_a, ot_b, ot_c, wg[l], row(b_gate[l]), wbr[l], wo[l],
                          row(ln2_g[l]), row(ln2_b[l]))
        h, hb = _ffn_ln(h, f2w1[l], f2w2[l], row(ln3_g[l]), row(ln3_b[l]))
    return h.reshape(bsz, seq, d)
```

```python
import functools
import math

import numpy as np
import jax
import jax.numpy as jnp
from jax import lax
from jax.experimental import pallas as pl
from jax.experimental.pallas import tpu as pltpu

F32 = jnp.float32
BF16 = jnp.bfloat16

D_MODEL = 1024
DEPTH = 4
HEAD_DIM = 64
NSA_HEADS = 8
NSA_GROUPS = 2
HPG = NSA_HEADS // NSA_GROUPS
CMP_BLOCK = 32
CMP_STRIDE = 16
CMP_HIDDEN = 256
SEL_BLOCK = 64
SEL_TOPK = 8
NSA_WINDOW = 512
SWA_HEADS = 8
SWA_GROUPS = 2
SWA_WINDOW = 128
FOX_HEADS = 8
Q_BLOCK = 128
REL_BUCKETS = 32
REL_MAX_DIST = 128
D_FF = 2816
LN_EPS = 1e-5
DN_ALPHA = (2 * DEPTH) ** 0.25
SCALE = HEAD_DIM ** -0.5
LOG2E = math.log2(math.e)
Q_SCALE = SCALE * LOG2E
SUM_ROWS = 16

NEG = -1e30
NEG_TEST = -1e29
LANES = 128
VMEM_LIMIT = 56 * 1024 * 1024

ROW_QA, ROW_QB, ROW_QC = 0, 512, 1024
ROW_VSLC, ROW_VWIN, ROW_VB, ROW_VC = 1536, 1664, 1792, 1920
ROWS_FEAT = 2432
CB_KC, CB_KSLC, CB_KWIN, CB_KB = 0, 8, 10, 12
COLS_K = 14 * LANES
N_GA = 3 * NSA_HEADS
COL_FC = N_GA
ROWS_SMALL = 32
LANE_C = HEAD_DIM


def _params(sem, vmem=VMEM_LIMIT):
    return pltpu.CompilerParams(dimension_semantics=sem, vmem_limit_bytes=vmem)


def _dot(a, b):
    return jnp.dot(a, b, preferred_element_type=F32)


def _dot_nt(a, b):
    return lax.dot_general(a, b, (((1,), (1,)), ((), ())), preferred_element_type=F32)


def _dot_tn(a, b):
    return lax.dot_general(a, b, (((0,), (0,)), ((), ())), preferred_element_type=F32)


def _layer_norm(y, g, b):
    mu = jnp.mean(y, axis=-1, keepdims=True)
    yc = y - mu
    var = jnp.mean(yc * yc, axis=-1, keepdims=True)
    return yc * lax.rsqrt(var + LN_EPS) * g + b


def _bucket_np(n):
    n = np.maximum(n, 0)
    exact = REL_BUCKETS // 2
    ratio = np.log(np.maximum(n, 1).astype(np.float64) / exact) / math.log(REL_MAX_DIST / exact)
    large = exact + (ratio * (REL_BUCKETS - exact)).astype(np.int64)
    return np.where(n < exact, n, np.minimum(large, REL_BUCKETS - 1)).astype(np.int32)


def _static_tables(seq):
    nch = seq // CMP_STRIDE
    ns = seq // SEL_BLOCK
    r = np.arange(Q_BLOCK)[None, :]
    u = np.arange(2 * nch)[:, None]
    n_c = r - (CMP_BLOCK - 1) - CMP_STRIDE * (u - nch)
    bk_c = np.where(n_c >= 0, _bucket_np(n_c), -1).astype(np.int32)
    c = np.arange(Q_BLOCK)[:, None]
    d_prev = Q_BLOCK + r - c
    d_diag = r - c
    bk_prev = _bucket_np(d_prev)
    bk_diag = np.where(d_diag >= 0, _bucket_np(d_diag), -1)
    bk_near = np.concatenate([bk_prev, bk_diag], axis=0).astype(np.int32)
    bk_swa = np.concatenate([np.where(d_prev < SWA_WINDOW, bk_prev, -1), bk_diag], axis=0).astype(np.int32)
    band0 = np.where(c > r, 0.0, NEG).astype(np.float32)
    i = np.arange(nch)[None, :]
    j = np.arange(64)[:, None]
    ov = (CMP_STRIDE * i < SEL_BLOCK * j + SEL_BLOCK) & (CMP_STRIDE * i + CMP_BLOCK - 1 >= SEL_BLOCK * j)
    ov &= (i < nch - 1) & (j < ns)
    s = np.arange(seq)[:, None]
    lane = np.arange(LANES)[None, :]
    onehot = (lane == HEAD_DIM + s // SEL_BLOCK).astype(np.float32)
    return dict(bk_c=bk_c, bk_near=bk_near, bk_swa=bk_swa, band0=band0,
                ovT=ov.astype(np.float32), onehot=onehot)


def _tables_kernel(rel_ref, bkc_ref, bkn_ref, bks_ref, u_ref, bn_ref, bs_ref):
    h = pl.program_id(0)

    def build(bk, col, delta):
        base = rel_ref[REL_BUCKETS - 1, col] if delta else 0.0
        acc = jnp.zeros(bk.shape, F32)
        for b in range(REL_BUCKETS):
            acc = jnp.where(bk == b, (rel_ref[b, col] - base) * LOG2E, acc)
        return jnp.where(bk < 0, NEG, acc)

    u_ref[0] = build(bkc_ref[...], h, True)
    bn_ref[0] = build(bkn_ref[...], h, True)
    bs_ref[0] = build(bks_ref[...], NSA_HEADS + h, False)


def _bias_tables(rel_bias, tabs):
    nch2 = tabs["bk_c"].shape[0]
    full = lambda shp: pl.BlockSpec(shp, lambda h: (0,) * len(shp))
    return pl.pallas_call(
        _tables_kernel,
        name="bias_tables",
        grid=(NSA_HEADS,),
        in_specs=[pl.BlockSpec(memory_space=pltpu.SMEM),
                  full((nch2, LANES)), full((2 * Q_BLOCK, LANES)), full((2 * Q_BLOCK, LANES))],
        out_specs=[pl.BlockSpec((1, nch2, LANES), lambda h: (h, 0, 0)),
                   pl.BlockSpec((1, 2 * Q_BLOCK, LANES), lambda h: (h, 0, 0)),
                   pl.BlockSpec((1, 2 * Q_BLOCK, LANES), lambda h: (h, 0, 0))],
        out_shape=[jax.ShapeDtypeStruct((NSA_HEADS, nch2, LANES), F32),
                   jax.ShapeDtypeStruct((NSA_HEADS, 2 * Q_BLOCK, LANES), F32),
                   jax.ShapeDtypeStruct((SWA_HEADS, 2 * Q_BLOCK, LANES), F32)],
        compiler_params=_params(("arbitrary",)),
    )(rel_bias, jnp.asarray(tabs["bk_c"]), jnp.asarray(tabs["bk_near"]), jnp.asarray(tabs["bk_swa"]))


FFN_TM = 1024
FFN_TF = 256


def _ffn_ln_kernel(x_ref, w1g_ref, w1u_ref, w2_ref, g_ref, b_ref, o_ref, ob_ref, acc_ref, xb_ref):
    j = pl.program_id(1)

    @pl.when(j == 0)
    def _init():
        acc_ref[...] = jnp.zeros_like(acc_ref)
        xb_ref[...] = x_ref[...].astype(BF16)

    xb = xb_ref[...]
    gt = _dot(xb, w1g_ref[...])
    up = _dot(xb, w1u_ref[...])
    act = (gt * jax.nn.sigmoid(gt) * up).astype(BF16)
    acc_ref[...] += _dot(act, w2_ref[...])

    @pl.when(j == pl.num_programs(1) - 1)
    def _finish():
        y = DN_ALPHA * x_ref[...] + 0.5 * acc_ref[...]
        out = _layer_norm(y, g_ref[...], b_ref[...])
        o_ref[...] = out
        ob_ref[...] = out.astype(BF16)


def _ffn_ln(x, w1, w2, g, b):
    n, d = x.shape
    f = w2.shape[0]
    tm = min(FFN_TM, n)
    nf = f // FFN_TF
    return pl.pallas_call(
        _ffn_ln_kernel,
        name="ffn_ln",
        grid=(n // tm, nf),
        in_specs=[pl.BlockSpec((tm, d), lambda i, j: (i, 0)),
                  pl.BlockSpec((d, FFN_TF), lambda i, j: (0, j)),
                  pl.BlockSpec((d, FFN_TF), lambda i, j: (0, j + nf)),
                  pl.BlockSpec((FFN_TF, d), lambda i, j: (j, 0)),
                  pl.BlockSpec((1, d), lambda i, j: (0, 0)),
                  pl.BlockSpec((1, d), lambda i, j: (0, 0))],
        out_specs=[pl.BlockSpec((tm, d), lambda i, j: (i, 0)),
                   pl.BlockSpec((tm, d), lambda i, j: (i, 0))],
        out_shape=[jax.ShapeDtypeStruct((n, d), F32), jax.ShapeDtypeStruct((n, d), BF16)],
        scratch_shapes=[pltpu.VMEM((tm, d), F32), pltpu.VMEM((tm, d), BF16)],
        compiler_params=_params(("arbitrary", "arbitrary")),
    )(x, w1, w1, w2, g, b)


PROJ_TM = 512


def _proj_kernel(xb_ref, wft_ref, wk_ref, wc_ref, ws_ref, wst_ref, zt_ref, zk_ref, zc_ref, fz_ref, gt_ref):
    xb = xb_ref[...]
    zt_ref[:ROW_VSLC, :] = (_dot_nt(wft_ref[:ROW_VSLC, :], xb) * Q_SCALE).astype(BF16)
    zt_ref[ROW_VSLC:, :] = _dot_nt(wft_ref[ROW_VSLC:, :], xb).astype(BF16)
    zk_ref[...] = _dot(xb, wk_ref[...]).astype(BF16)
    zc_ref[...] = _dot(xb, wc_ref[...]).astype(BF16)
    fz_ref[...] = _dot(xb, ws_ref[...])
    gt_ref[...] = _dot_nt(wst_ref[...], xb)


def _project(xb, wft, wk, wc, ws, wst):
    n, d = xb.shape
    tm = min(PROJ_TM, n)
    res = lambda shp: pl.BlockSpec(shp, lambda i: (0, 0))
    return pl.pallas_call(
        _proj_kernel,
        name="proj_in",
        grid=(n // tm,),
        in_specs=[pl.BlockSpec((tm, d), lambda i: (i, 0)),
                  res(wft.shape), res(wk.shape), res(wc.shape), res(ws.shape), res(wst.shape)],
        out_specs=[pl.BlockSpec((ROWS_FEAT, tm), lambda i: (0, i)),
                   pl.BlockSpec((tm, COLS_K), lambda i: (i, 0)),
                   pl.BlockSpec((tm, 4 * HEAD_DIM), lambda i: (i, 0)),
                   pl.BlockSpec((tm, LANES), lambda i: (i, 0)),
                   pl.BlockSpec((ROWS_SMALL, tm), lambda i: (0, i))],
        out_shape=[jax.ShapeDtypeStruct((ROWS_FEAT, n), BF16),
                   jax.ShapeDtypeStruct((n, COLS_K), BF16),
                   jax.ShapeDtypeStruct((n, 4 * HEAD_DIM), BF16),
                   jax.ShapeDtypeStruct((n, LANES), F32),
                   jax.ShapeDtypeStruct((ROWS_SMALL, n), F32)],
        compiler_params=_params(("arbitrary",)),
    )(xb, wft, wk, wc, ws, wst)


def _gelu_tanh(x):
    return 0.5 * x * (1.0 + jnp.tanh(math.sqrt(2.0 / math.pi) * (x + 0.044715 * (x * x * x))))


def _compress_kernel(ck_ref, cv_ref, pek_ref, pev_ref, wk1_ref, wk2_ref, wv1_ref, wv2t_ref, kc_ref, vct_ref):
    half = CMP_STRIDE * HEAD_DIM
    nch = ck_ref.shape[2]

    def hidden(c_ref, pe_ref, w1_ref):
        c = c_ref[0, 0].astype(F32)
        top = (c + pe_ref[:, :half]).astype(BF16)
        bot = (c + pe_ref[:, half:]).astype(BF16)
        a = _dot(top, w1_ref[:half, :])
        bm = _dot(bot, w1_ref[half:, :])
        h = a + pltpu.roll(bm, nch - 1, axis=0)
        return _gelu_tanh(h).astype(BF16)

    kc = _dot(hidden(ck_ref, pek_ref, wk1_ref), wk2_ref[...])
    kc_ref[0, 0] = jnp.concatenate([kc, jnp.zeros_like(kc)], axis=1).astype(BF16)
    vct_ref[0, 0] = _dot_nt(wv2t_ref[...], hidden(cv_ref, pev_ref, wv1_ref)).astype(BF16)


def _compress(zc4, pek, pev, wk1, wk2, wv1, wv2t):
    bsz, _, nch, wide = zc4.shape
    res = lambda shp: pl.BlockSpec(shp, lambda b, g: (0, 0))
    return pl.pallas_call(
        _compress_kernel,
        name="nsa_compress",
        grid=(bsz, NSA_GROUPS),
        in_specs=[pl.BlockSpec((1, 1, nch, wide), lambda b, g: (b, g, 0, 0)),
                  pl.BlockSpec((1, 1, nch, wide), lambda b, g: (b, NSA_GROUPS + g, 0, 0)),
                  res(pek.shape), res(pev.shape), res(wk1.shape), res(wk2.shape), res(wv1.shape), res(wv2t.shape)],
        out_specs=[pl.BlockSpec((1, 1, nch, LANES), lambda b, g: (b, g, 0, 0)),
                   pl.BlockSpec((1, 1, HEAD_DIM, nch), lambda b, g: (b, g, 0, 0))],
        out_shape=[jax.ShapeDtypeStruct((bsz, NSA_GROUPS, nch, LANES), BF16),
                   jax.ShapeDtypeStruct((bsz, NSA_GROUPS, HEAD_DIM, nch), BF16)],
        compiler_params=_params(("arbitrary", "arbitrary")),
    )(zc4, zc4, pek, pev, wk1, wk2, wv1, wv2t)


SEL_TK = 256


def _with_sum_rows(vt):
    return jnp.concatenate([vt, jnp.ones((SUM_ROWS, vt.shape[1]), vt.dtype)], axis=0)


def _softmax_update(s, m, acc, vt):
    mn = jnp.maximum(m, jnp.max(s, axis=0, keepdims=True))
    p = jnp.exp2(s - mn).astype(BF16)
    acc = jnp.exp2(m - mn) * acc + _dot(_with_sum_rows(vt), p)
    return mn, acc


def _softmax_init(width):
    return jnp.full((1, width), -jnp.inf, F32), jnp.zeros((HEAD_DIM + SUM_ROWS, width), F32)


def _softmax_finish(acc):
    return acc[:HEAD_DIM] * (1.0 / acc[HEAD_DIM:HEAD_DIM + 1])


def _nsa_kernel(q_ref, gate_ref, kc_ref, vct_ref, ks_ref, vst_ref, kw_ref, vwt_ref, oh_ref, ovt_ref,
                u_ref, bn_ref, band_ref, o_ref, sa_ref, sb_ref):
    g = pl.program_id(1)
    qb = pl.program_id(2)
    nch = kc_ref.shape[2]
    seq = ks_ref.shape[0]
    tq = Q_BLOCK
    wide = HPG * tq

    def lanes4(fn):
        return jnp.concatenate([fn(hp) for hp in range(HPG)], axis=1)

    qs = [q_ref[hp * HEAD_DIM:(hp + 1) * HEAD_DIM, :] for hp in range(HPG)]
    zeros_q = jnp.zeros((HEAD_DIM, tq), BF16)
    q_pad = lanes4(lambda hp: jnp.concatenate([qs[hp], zeros_q], axis=0))

    off = pl.multiple_of(nch - 8 * qb, 8)
    s_c = _dot(kc_ref[0, 0], q_pad) + lanes4(lambda hp: u_ref[hp, pl.ds(off, nch), :])
    m_c = jnp.max(s_c, axis=0, keepdims=True)
    p_c = jnp.where(s_c > NEG_TEST, jnp.exp2(s_c - m_c), 0.0)
    l_c = jnp.sum(p_c, axis=0, keepdims=True)
    p_c = p_c * (1.0 / jnp.maximum(l_c, 1e-30))
    o_c = _dot(vct_ref[0, 0], p_c.astype(BF16))

    p_sum = p_c[:, 0:tq]
    for hp in range(1, HPG):
        p_sum = p_sum + p_c[:, hp * tq:(hp + 1) * tq]
    imp = jnp.dot(ovt_ref[...], p_sum, preferred_element_type=F32, precision=lax.Precision.HIGHEST)
    nsp = imp.shape[0]
    blk = lax.broadcasted_iota(jnp.int32, (nsp, tq), 0).astype(F32)
    r_lane = lax.broadcasted_iota(jnp.int32, (nsp, tq), 1)
    cur = (2 * qb).astype(F32) + jnp.where(r_lane >= SEL_BLOCK, 1.0, 0.0)
    forced = (blk == 0.0) | (blk == cur) | (blk == cur - 1.0)
    valid = blk <= cur
    score = jnp.where(forced, jnp.inf, jnp.where(valid, imp, -jnp.inf))
    sel = jnp.zeros((nsp, tq), jnp.bool_)
    for _ in range(SEL_TOPK):
        mx = jnp.max(score, axis=0, keepdims=True)
        first = jnp.min(jnp.where(score == mx, blk, 1e9), axis=0, keepdims=True)
        hit = blk == first
        sel = sel | hit
        score = jnp.where(hit, -jnp.inf, score)
    sel = sel & valid
    selb = jnp.where(sel, 0.0, NEG)
    far_lim = (2 * qb - 2).astype(F32)
    selb_far = jnp.where(blk < far_lim, selb, NEG).astype(BF16)
    selb = selb.astype(BF16)
    q_near = lanes4(lambda hp: jnp.concatenate([qs[hp], selb], axis=0))
    q_far = lanes4(lambda hp: jnp.concatenate([qs[hp], selb_far], axis=0))

    bn_prev = lanes4(lambda hp: bn_ref[hp, 0:tq, :])
    bn_diag = lanes4(lambda hp: bn_ref[hp, tq:2 * tq, :])

    def far_scores(tile, s_ref):
        k0 = pl.multiple_of(jnp.minimum(tile * SEL_TK, seq - SEL_TK), SEL_TK)
        s_ref[...] = _dot(ks_ref[pl.ds(k0, SEL_TK), :] + oh_ref[pl.ds(k0, SEL_TK), :], q_far)

    def far_consume(s_ref, tile, carry):
        k0 = pl.multiple_of(tile * SEL_TK, SEL_TK)
        return _softmax_update(s_ref[...], *carry, vst_ref[:, pl.ds(k0, SEL_TK)])

    def far_body(j, carry):
        far_scores(2 * j + 1, sb_ref)
        carry = far_consume(sa_ref, 2 * j, carry)
        far_scores(2 * j + 2, sa_ref)
        return far_consume(sb_ref, 2 * j + 1, carry)

    far_tiles = 2 * SEL_TK // tq
    n_far = (jnp.maximum(qb - 1, 0) + far_tiles - 1) // far_tiles
    far_scores(0, sa_ref)
    m_s, acc_s = lax.fori_loop(0, n_far, far_body, _softmax_init(wide))

    p0 = pl.multiple_of(jnp.maximum(qb - 1, 0) * tq, tq)
    d0 = pl.multiple_of(qb * tq, tq)
    kp = ks_ref[pl.ds(p0, tq), :] + oh_ref[pl.ds(p0, tq), :]
    kd = ks_ref[pl.ds(d0, tq), :] + oh_ref[pl.ds(d0, tq), :]
    s_p = jnp.where(qb > 0, _dot(kp, q_near) + bn_prev, NEG)
    s_d = _dot(kd, q_near) + bn_diag
    s_n = jnp.concatenate([s_p, s_d], axis=0)
    vt_n = jnp.concatenate([vst_ref[:, pl.ds(p0, tq)], vst_ref[:, pl.ds(d0, tq)]], axis=1)
    _, acc_s = _softmax_update(s_n, m_s, acc_s, vt_n)
    o_s = _softmax_finish(acc_s)

    n_win = NSA_WINDOW // tq
    band4 = lanes4(lambda hp: band_ref[...])
    s_tiles, v_tiles = [], []
    for a in range(n_win + 1):
        ti = qb - n_win + a
        t0 = pl.multiple_of(jnp.maximum(ti, 0) * tq, tq)
        s_a = _dot(kw_ref[pl.ds(t0, tq), :], q_pad)
        if a == 0:
            s_a = s_a + band4
        if a == n_win - 1:
            s_a = s_a + bn_prev
        if a == n_win:
            s_a = s_a + bn_diag
        else:
            s_a = jnp.where(ti >= 0, s_a, NEG)
        s_tiles.append(s_a)
        v_tiles.append(vwt_ref[:, pl.ds(t0, tq)])
    s_w = jnp.concatenate(s_tiles, axis=0)
    m_w = jnp.max(s_w, axis=0, keepdims=True)
    p_w = jnp.exp2(s_w - m_w).astype(BF16)
    o_w = _softmax_finish(_dot(_with_sum_rows(jnp.concatenate(v_tiles, axis=1)), p_w))

    for hp in range(HPG):
        row = g * (HPG * 3) + hp * 3
        gates = jax.nn.sigmoid(gate_ref[pl.ds(row, 3), :])
        sl = slice(hp * tq, (hp + 1) * tq)
        out = gates[0:1] * o_c[:, sl] + gates[1:2] * o_s[:, sl] + gates[2:3] * o_w[:, sl]
        o_ref[hp * HEAD_DIM:(hp + 1) * HEAD_DIM, :] = out.astype(BF16)


def _nsa(zt, zk, gt, kc, vct, onehot, ovt, u_tab, bn_tab, band0, bsz, seq):
    nq = seq // Q_BLOCK
    nch = seq // CMP_STRIDE
    gw = HPG * HEAD_DIM
    n = bsz * seq
    const2 = lambda shp: pl.BlockSpec(shp, lambda b, g, q: (0, 0))
    return pl.pallas_call(
        _nsa_kernel,
        name="nsa_attn",
        grid=(bsz, NSA_GROUPS, nq),
        in_specs=[pl.BlockSpec((gw, Q_BLOCK), lambda b, g, q: (ROW_QA // gw + g, b * nq + q)),
                  pl.BlockSpec((ROWS_SMALL, Q_BLOCK), lambda b, g, q: (0, b * nq + q)),
                  pl.BlockSpec((1, 1, nch, LANES), lambda b, g, q: (b, g, 0, 0)),
                  pl.BlockSpec((1, 1, HEAD_DIM, nch), lambda b, g, q: (b, g, 0, 0)),
                  pl.BlockSpec((seq, LANES), lambda b, g, q: (b, CB_KSLC + g)),
                  pl.BlockSpec((HEAD_DIM, seq), lambda b, g, q: (ROW_VSLC // HEAD_DIM + g, b)),
                  pl.BlockSpec((seq, LANES), lambda b, g, q: (b, CB_KWIN + g)),
                  pl.BlockSpec((HEAD_DIM, seq), lambda b, g, q: (ROW_VWIN // HEAD_DIM + g, b)),
                  const2(onehot.shape), const2(ovt.shape),
                  pl.BlockSpec((HPG,) + u_tab.shape[1:], lambda b, g, q: (g, 0, 0)),
                  pl.BlockSpec((HPG,) + bn_tab.shape[1:], lambda b, g, q: (g, 0, 0)),
                  const2(band0.shape)],
        out_specs=pl.BlockSpec((gw, Q_BLOCK), lambda b, g, q: (g, b * nq + q)),
        out_shape=jax.ShapeDtypeStruct((NSA_HEADS * HEAD_DIM, n), BF16),
        scratch_shapes=[pltpu.VMEM((SEL_TK, HPG * Q_BLOCK), F32), pltpu.VMEM((SEL_TK, HPG * Q_BLOCK), F32)],
        compiler_params=_params(("arbitrary", "arbitrary", "arbitrary")),
    )(zt, gt, kc, vct, zk, zt, zk, zt, onehot, ovt, u_tab, bn_tab, band0)


SWA_SUB = 4


def _swa_kernel(sink_ref, q_ref, kcur_ref, kprev_ref, vcur_ref, vprev_ref, bs_ref, o_ref):
    g = pl.program_id(1)
    t = pl.program_id(2)
    tq = SWA_WINDOW
    zeros_q = jnp.zeros((HEAD_DIM, tq), BF16)

    def lanes4(fn):
        return jnp.concatenate([fn(hp) for hp in range(HPG)], axis=1)

    bs_prev = lanes4(lambda hp: bs_ref[hp, 0:tq, :])
    bs_diag = lanes4(lambda hp: bs_ref[hp, tq:2 * tq, :])
    sink = lanes4(lambda hp: jnp.full((1, tq), sink_ref[g * HPG + hp] * LOG2E, F32))
    for sub in range(SWA_SUB):
        cs = slice(sub * tq, (sub + 1) * tq)
        q_pad = lanes4(lambda hp: jnp.concatenate(
            [q_ref[hp * HEAD_DIM:(hp + 1) * HEAD_DIM, cs], zeros_q], axis=0))
        if sub == 0:
            kp, vp = kprev_ref[...], vprev_ref[...]
        else:
            ps = slice((sub - 1) * tq, sub * tq)
            kp, vp = kcur_ref[ps, :], vcur_ref[:, ps]
        s_p = _dot(kp, q_pad) + bs_prev
        if sub == 0:
            s_p = jnp.where(t > 0, s_p, NEG)
        s_d = _dot(kcur_ref[cs, :], q_pad) + bs_diag
        s = jnp.concatenate([s_p, s_d], axis=0)
        m = jnp.maximum(jnp.max(s, axis=0, keepdims=True), sink)
        p = jnp.exp2(s - m).astype(BF16)
        acc = _dot(_with_sum_rows(jnp.concatenate([vp, vcur_ref[:, cs]], axis=1)), p)
        o = acc[:HEAD_DIM] * (1.0 / (acc[HEAD_DIM:HEAD_DIM + 1] + jnp.exp2(sink - m)))
        for hp in range(HPG):
            o_ref[hp * HEAD_DIM:(hp + 1) * HEAD_DIM, cs] = o[:, hp * tq:(hp + 1) * tq].astype(BF16)


def _swa(zt, zk, sinks, bs_tab, bsz, seq):
    tq = SWA_WINDOW
    nq = seq // tq
    nt = nq // SWA_SUB
    wide = SWA_SUB * tq
    gw = HPG * HEAD_DIM
    n = bsz * seq
    prev = lambda b, t: b * nq + jnp.maximum(t * SWA_SUB - 1, 0)
    return pl.pallas_call(
        _swa_kernel,
        name="swa_attn",
        grid=(bsz, SWA_GROUPS, nt),
        in_specs=[pl.BlockSpec(memory_space=pltpu.SMEM),
                  pl.BlockSpec((gw, wide), lambda b, g, t: (ROW_QB // gw + g, b * nt + t)),
                  pl.BlockSpec((wide, LANES), lambda b, g, t: (b * nt + t, CB_KB + g)),
                  pl.BlockSpec((tq, LANES), lambda b, g, t: (prev(b, t), CB_KB + g)),
                  pl.BlockSpec((HEAD_DIM, wide), lambda b, g, t: (ROW_VB // HEAD_DIM + g, b * nt + t)),
                  pl.BlockSpec((HEAD_DIM, tq), lambda b, g, t: (ROW_VB // HEAD_DIM + g, prev(b, t))),
                  pl.BlockSpec((HPG,) + bs_tab.shape[1:], lambda b, g, t: (g, 0, 0))],
        out_specs=pl.BlockSpec((gw, wide), lambda b, g, t: (g, b * nt + t)),
        out_shape=jax.ShapeDtypeStruct((SWA_HEADS * HEAD_DIM, n), BF16),
        compiler_params=_params(("arbitrary", "arbitrary", "arbitrary")),
    )(sinks, zt, zk, zk, zt, zt, bs_tab)


def _fox_keys_kernel(fz_ref, bf_ref, k_ref, o_ref, c_ref):
    seq = fz_ref.shape[0]
    ch = LANES
    logf = jax.nn.log_sigmoid(fz_ref[...] + bf_ref[...])
    tri = (lax.broadcasted_iota(jnp.int32, (ch, ch), 0) >= lax.broadcasted_iota(jnp.int32, (ch, ch), 1)).astype(F32)
    carry = jnp.zeros((1, LANES), F32)
    for i in range(seq // ch):
        c = jnp.dot(tri, logf[i * ch:(i + 1) * ch, :], preferred_element_type=F32,
                    precision=lax.Precision.HIGHEST) + carry
        c_ref[i * ch:(i + 1) * ch, :] = c
        carry = c[ch - 1:ch, :]
    neg = c_ref[...] * (-LOG2E)
    hi = neg.astype(BF16)
    r1 = neg - hi.astype(F32)
    mid = r1.astype(BF16)
    lo = (r1 - mid.astype(F32)).astype(BF16)
    parts = jnp.concatenate([hi, mid, lo], axis=1)
    row = lax.broadcasted_iota(jnp.int32, (3 * LANES, LANES), 0)
    col = lax.broadcasted_iota(jnp.int32, (3 * LANES, LANES), 1)
    lane = lax.broadcasted_iota(jnp.int32, (seq, LANES), 1)
    for h in range(FOX_HEADS):
        src = COL_FC + h
        place = ((row == src) & (col == LANE_C)) | ((row == LANES + src) & (col == LANE_C + 1)) \
            | ((row == 2 * LANES + src) & (col == LANE_C + 2))
        bias = _dot(parts, place.astype(BF16)).astype(BF16)
        hs = slice(h * LANES, (h + 1) * LANES)
        o_ref[:, hs] = jnp.where(lane < HEAD_DIM, k_ref[:, hs], bias)


def _fox_keys(fz, bf_row, zk, bsz, seq):
    wide = FOX_HEADS * LANES
    return pl.pallas_call(
        _fox_keys_kernel,
        name="fox_keys",
        grid=(bsz,),
        in_specs=[pl.BlockSpec((seq, LANES), lambda b: (b, 0)),
                  pl.BlockSpec((1, LANES), lambda b: (0, 0)),
                  pl.BlockSpec((seq, wide), lambda b: (b, CB_KC))],
        out_specs=pl.BlockSpec((seq, wide), lambda b: (b, 0)),
        out_shape=jax.ShapeDtypeStruct((bsz * seq, wide), BF16),
        scratch_shapes=[pltpu.VMEM((seq, LANES), F32)],
        compiler_params=_params(("arbitrary",)),
    )(fz, bf_row, zk)


FOX_T = 512
FOX_HPS = 2


def _fox_kernel(q_ref, k_ref, vt_ref, o_ref, *s_refs):
    qi = pl.program_id(2)
    t = q_ref.shape[1]
    tk = t // 2
    ones_rows = (lax.broadcasted_iota(jnp.int32, (HEAD_DIM, t), 0) < 3).astype(BF16)
    rows = lambda hh: slice(hh * HEAD_DIM, (hh + 1) * HEAD_DIM)
    q_aug = [jnp.concatenate([q_ref[rows(hh), :], ones_rows], axis=0) for hh in range(FOX_HPS)]

    def scores(tile, slot):
        k0 = pl.multiple_of(tile * tk, tk)
        for hh in range(FOX_HPS):
            s_refs[2 * hh + slot][...] = _dot(k_ref[pl.ds(k0, tk), hh * LANES:(hh + 1) * LANES], q_aug[hh])

    def consume(slot, tile, carry, first_masked_query=None):
        k0 = pl.multiple_of(tile * tk, tk)
        out = []
        for hh in range(FOX_HPS):
            s = s_refs[2 * hh + slot][...]
            if first_masked_query is not None:
                key = lax.broadcasted_iota(jnp.int32, (tk, t), 0) + first_masked_query
                s = jnp.where(key > lax.broadcasted_iota(jnp.int32, (tk, t), 1), NEG, s)
            out.append(_softmax_update(s, *carry[hh], vt_ref[rows(hh), pl.ds(k0, tk)]))
        return tuple(out)

    def body(j, carry):
        scores(2 * j + 1, 1)
        carry = consume(0, 2 * j, carry)
        scores(2 * j + 2, 0)
        return consume(1, 2 * j + 1, carry)

    scores(0, 0)
    carry = lax.fori_loop(0, qi, body, tuple(_softmax_init(t) for _ in range(FOX_HPS)))
    scores(2 * qi + 1, 1)
    carry = consume(0, 2 * qi, carry, 0)
    carry = consume(1, 2 * qi + 1, carry, tk)
    for hh in range(FOX_HPS):
        o_ref[rows(hh), :] = _softmax_finish(carry[hh][1]).astype(BF16)


def _fox(zt, kaug, bsz, seq):
    t = min(FOX_T, seq)
    nq = seq // t
    n = bsz * seq
    hw = FOX_HPS * HEAD_DIM
    return pl.pallas_call(
        _fox_kernel,
        name="fox_attn",
        grid=(bsz, FOX_HEADS // FOX_HPS, nq),
        in_specs=[pl.BlockSpec((hw, t), lambda b, h, q: (ROW_QC // hw + h, b * nq + q)),
                  pl.BlockSpec((seq, FOX_HPS * LANES), lambda b, h, q: (b, h)),
                  pl.BlockSpec((hw, seq), lambda b, h, q: (ROW_VC // hw + h, b))],
        out_specs=pl.BlockSpec((hw, t), lambda b, h, q: (h, b * nq + q)),
        out_shape=jax.ShapeDtypeStruct((FOX_HEADS * HEAD_DIM, n), BF16),
        scratch_shapes=[pltpu.VMEM((t // 2, t), F32) for _ in range(2 * FOX_HPS)],
        compiler_params=_params(("arbitrary", "arbitrary", "arbitrary")),
    )(zt, kaug, zt)


MERGE_TM = 512


def _merge_ln_kernel(x_ref, xb_ref, oa_ref, ob_ref, oc_ref, wg_ref, bg_ref, wbr_ref, wo_ref, g_ref, b_ref,
                     o_ref, obf_ref):
    d = x_ref.shape[1]
    xb = xb_ref[...]
    merged = None
    for i, br_ref in enumerate((oa_ref, ob_ref, oc_ref)):
        gate = jax.nn.sigmoid(_dot(xb, wg_ref[:, i * d:(i + 1) * d]) + bg_ref[:, i * d:(i + 1) * d])
        term = gate * _dot_tn(br_ref[...], wbr_ref[i])
        merged = term if merged is None else merged + term
    mix = _dot(merged.astype(BF16), wo_ref[...])
    out = _layer_norm(DN_ALPHA * x_ref[...] + mix, g_ref[...], b_ref[...])
    o_ref[...] = out
    obf_ref[...] = out.astype(BF16)


def _merge_ln(x, xb, ota, otb, otc, wg, bg, wbr, wo, g, b):
    n, d = x.shape
    tm = min(MERGE_TM, n)
    w = ota.shape[0]
    tok = pl.BlockSpec((tm, d), lambda i: (i, 0))
    feat = pl.BlockSpec((w, tm), lambda i: (0, i))
    res2 = lambda shp: pl.BlockSpec(shp, lambda i: (0, 0))
    return pl.pallas_call(
        _merge_ln_kernel,
        name="merge_ln",
        grid=(n // tm,),
        in_specs=[tok, tok, feat, feat, feat, res2(wg.shape), res2(bg.shape),
                  pl.BlockSpec(wbr.shape, lambda i: (0, 0, 0)), res2(wo.shape), res2(g.shape), res2(b.shape)],
        out_specs=[tok, tok],
        out_shape=[jax.ShapeDtypeStruct((n, d), F32), jax.ShapeDtypeStruct((n, d), BF16)],
        compiler_params=_params(("arbitrary",)),
    )(x, xb, ota, otb, otc, wg, bg, wbr, wo, g, b)


def _split_w_in(w_in):
    sizes = [NSA_HEADS * HEAD_DIM, 6 * NSA_GROUPS * HEAD_DIM, N_GA, SWA_HEADS * HEAD_DIM,
             SWA_GROUPS * HEAD_DIM, SWA_GROUPS * HEAD_DIM, FOX_HEADS * HEAD_DIM, FOX_HEADS * HEAD_DIM,
             FOX_HEADS * HEAD_DIM, FOX_HEADS]
    q_a, kv_a, g_a, q_b, k_b, v_b, q_c, k_c, v_c, f_c = jnp.split(w_in, np.cumsum(sizes)[:-1].tolist(), axis=-1)
    k_cmp, v_cmp, k_slc, v_slc, k_win, v_win = jnp.split(kv_a, 6, axis=-1)
    depth, d = w_in.shape[0], w_in.shape[1]

    def pad_heads(w):
        nh = w.shape[-1] // HEAD_DIM
        w = w.reshape(depth, d, nh, HEAD_DIM)
        return jnp.concatenate([w, jnp.zeros_like(w)], axis=-1).reshape(depth, d, nh * LANES)

    w_feat_t = jnp.swapaxes(jnp.concatenate([q_a, q_b, q_c, v_slc, v_win, v_b, v_c], axis=-1), 1, 2)
    w_k = jnp.concatenate([pad_heads(k_c), pad_heads(k_slc), pad_heads(k_win), pad_heads(k_b)], axis=-1)
    w_cmp = jnp.concatenate([k_cmp, v_cmp], axis=-1)
    small = jnp.concatenate([g_a, f_c], axis=-1)
    w_small = jnp.concatenate([small, jnp.zeros((depth, d, LANES - ROWS_SMALL), small.dtype)], axis=-1)
    w_small_t = jnp.swapaxes(small, 1, 2)
    return w_feat_t, w_k, w_cmp, w_small, w_small_t


def kernel(x, rel_bias, ln1_g, ln1_b, ffn1_w1, ffn1_w2, w_in, cmp_pe_k, cmp_pe_v, cmp_k_w1, cmp_k_w2, cmp_v_w1,
           cmp_v_w2, swa_sinks, fox_b_f, w_br_a, w_br_b, w_br_c, w_gate, b_gate, w_out, ln2_g, ln2_b, ffn2_w1,
           ffn2_w2, ln3_g, ln3_b):
    bsz, seq, d = x.shape
    n = bsz * seq
    depth = w_in.shape[0]
    nch = seq // CMP_STRIDE
    tabs = _static_tables(seq)
    onehot = jnp.asarray(tabs["onehot"], BF16)
    ovt = jnp.asarray(tabs["ovT"])
    band0 = jnp.asarray(tabs["band0"])
    u_tab, bn_tab, bs_tab = _bias_tables(rel_bias, tabs)

    bf = lambda w: w.astype(BF16)
    w_feat_t, w_k, w_cmp, w_small, w_small_t = [bf(w) for w in _split_w_in(w_in)]
    f1w1, f1w2, f2w1, f2w2 = bf(ffn1_w1), bf(ffn1_w2), bf(ffn2_w1), bf(ffn2_w2)
    ck1, ck2, cv1 = bf(cmp_k_w1), bf(cmp_k_w2), bf(cmp_v_w1)
    cv2t = bf(jnp.swapaxes(cmp_v_w2, 1, 2))
    pek = cmp_pe_k.reshape(depth, 1, CMP_BLOCK * HEAD_DIM)
    pev = cmp_pe_v.reshape(depth, 1, CMP_BLOCK * HEAD_DIM)
    wg, wo = bf(w_gate), bf(w_out)
    wbr = bf(jnp.stack([w_br_a, w_br_b, w_br_c], axis=1))
    bf_rows = jnp.zeros((depth, 1, LANES), F32).at[:, 0, COL_FC:COL_FC + FOX_HEADS].set(fox_b_f)
    row = lambda v: v.reshape(1, -1)

    h = x.reshape(n, d)
    for l in range(depth):
        h, hb = _ffn_ln(h, f1w1[l], f1w2[l], row(ln1_g[l]), row(ln1_b[l]))
        zt, zk, zc, fz, gt = _project(hb, w_feat_t[l], w_k[l], w_cmp[l], w_small[l], w_small_t[l])
        zc4 = zc.reshape(bsz, nch, CMP_STRIDE, 2 * NSA_GROUPS, HEAD_DIM).transpose(0, 3, 1, 2, 4)
        zc4 = zc4.reshape(bsz, 2 * NSA_GROUPS, nch, CMP_STRIDE * HEAD_DIM)
        kc, vct = _compress(zc4, pek[l], pev[l], ck1[l], ck2[l], cv1[l], cv2t[l])
        ot_a = _nsa(zt, zk, gt, kc, vct, onehot, ovt, u_tab, bn_tab, band0, bsz, seq)
        ot_b = _swa(zt, zk, swa_sinks[l], bs_tab, bsz, seq)
        kaug = _fox_keys(fz, bf_rows[l], zk, bsz, seq)
        ot_c = _fox(zt, kaug, bsz, seq)
        h, hb = _merge_ln(h, hb, ot_a, ot_b, ot_c, wg[l], row(b_gate[l]), wbr[l], wo[l],
                          row(ln2_g[l]), row(ln2_b[l]))
        h, hb = _ffn_ln(h, f2w1[l], f2w2[l], row(ln3_g[l]), row(ln3_b[l]))
    return h.reshape(bsz, seq, d)
```

```python
import functools
import math

import numpy as np
import jax
import jax.numpy as jnp
from jax import lax
from jax.experimental import pallas as pl
from jax.experimental.pallas import tpu as pltpu

F32 = jnp.float32
BF16 = jnp.bfloat16

D_MODEL = 1024
DEPTH = 4
HEAD_DIM = 64
NSA_HEADS = 8
NSA_GROUPS = 2
HPG = NSA_HEADS // NSA_GROUPS
CMP_BLOCK = 32
CMP_STRIDE = 16
CMP_HIDDEN = 256
SEL_BLOCK = 64
SEL_TOPK = 8
NSA_WINDOW = 512
SWA_HEADS = 8
SWA_GROUPS = 2
SWA_WINDOW = 128
FOX_HEADS = 8
NSA_TQ = 256
REL_BUCKETS = 32
REL_MAX_DIST = 128
D_FF = 2816
LN_EPS = 1e-5
DN_ALPHA = (2 * DEPTH) ** 0.25
SCALE = HEAD_DIM ** -0.5
LOG2E = math.log2(math.e)
Q_SCALE = SCALE * LOG2E
SUM_ROWS = 16

NEG = -1e30
NEG_TEST = -1e29
LANES = 128
VMEM_LIMIT = 56 * 1024 * 1024

ROW_QA, ROW_QB, ROW_QC = 0, 512, 1024
ROW_VSLC, ROW_VWIN, ROW_VB, ROW_VC = 1536, 1664, 1792, 1920
ROWS_FEAT = 2432
CB_KC, CB_KSLC, CB_KWIN, CB_KB = 0, 8, 10, 12
COLS_K = 14 * LANES
N_GA = 3 * NSA_HEADS
COL_FC = N_GA
ROWS_SMALL = 32
LANE_C = HEAD_DIM


def _params(sem, vmem=VMEM_LIMIT):
    return pltpu.CompilerParams(dimension_semantics=sem, vmem_limit_bytes=vmem)


def _dot(a, b):
    return jnp.dot(a, b, preferred_element_type=F32)


def _dot_nt(a, b):
    return lax.dot_general(a, b, (((1,), (1,)), ((), ())), preferred_element_type=F32)


def _dot_tn(a, b):
    return lax.dot_general(a, b, (((0,), (0,)), ((), ())), preferred_element_type=F32)


def _layer_norm(y, g, b):
    mu = jnp.mean(y, axis=-1, keepdims=True)
    yc = y - mu
    var = jnp.mean(yc * yc, axis=-1, keepdims=True)
    return yc * lax.rsqrt(var + LN_EPS) * g + b


def _bucket_np(n):
    n = np.maximum(n, 0)
    exact = REL_BUCKETS // 2
    ratio = np.log(np.maximum(n, 1).astype(np.float64) / exact) / math.log(REL_MAX_DIST / exact)
    large = exact + (ratio * (REL_BUCKETS - exact)).astype(np.int64)
    return np.where(n < exact, n, np.minimum(large, REL_BUCKETS - 1)).astype(np.int32)


def _static_tables(seq):
    nch = seq // CMP_STRIDE
    ns = seq // SEL_BLOCK

    def near_buckets(tq):
        r = np.arange(tq)[None, :]
        c = np.arange(tq)[:, None]
        d_prev = tq + r - c
        d_diag = r - c
        return d_prev, _bucket_np(d_prev), np.where(d_diag >= 0, _bucket_np(d_diag), -1)

    r = np.arange(NSA_TQ)[None, :]
    u = np.arange(2 * nch)[:, None]
    n_c = r - (CMP_BLOCK - 1) - CMP_STRIDE * (u - nch)
    bk_c = np.where(n_c >= 0, _bucket_np(n_c), -1).astype(np.int32)
    _, bk_prev, bk_diag = near_buckets(NSA_TQ)
    bk_near = np.concatenate([bk_prev, bk_diag], axis=0).astype(np.int32)
    d_prev, bk_prev, bk_diag = near_buckets(SWA_WINDOW)
    bk_swa = np.concatenate([np.where(d_prev < SWA_WINDOW, bk_prev, -1), bk_diag], axis=0).astype(np.int32)
    band0 = np.where(np.arange(NSA_TQ)[:, None] > r, 0.0, NEG).astype(np.float32)
    i = np.arange(nch)[None, :]
    j = np.arange(64)[:, None]
    ov = (CMP_STRIDE * i < SEL_BLOCK * j + SEL_BLOCK) & (CMP_STRIDE * i + CMP_BLOCK - 1 >= SEL_BLOCK * j)
    ov &= (i < nch - 1) & (j < ns)
    s = np.arange(seq)[:, None]
    lane = np.arange(LANES)[None, :]
    onehot = (lane == HEAD_DIM + s // SEL_BLOCK).astype(np.float32)
    return dict(bk_c=bk_c, bk_near=bk_near, bk_swa=bk_swa, band0=band0,
                ovT=ov.astype(np.float32), onehot=onehot)


def _tables_kernel(rel_ref, bkc_ref, bkn_ref, bks_ref, u_ref, bn_ref, bs_ref):
    h = pl.program_id(0)

    def build(bk, col, delta):
        base = rel_ref[REL_BUCKETS - 1, col] if delta else 0.0
        acc = jnp.zeros(bk.shape, F32)
        for b in range(REL_BUCKETS):
            acc = jnp.where(bk == b, (rel_ref[b, col] - base) * LOG2E, acc)
        return jnp.where(bk < 0, NEG, acc)

    u_ref[0] = build(bkc_ref[...], h, True)
    bn_ref[0] = build(bkn_ref[...], h, True)
    bs_ref[0] = build(bks_ref[...], NSA_HEADS + h, False)


def _bias_tables(rel_bias, tabs):
    buckets = [jnp.asarray(tabs[k]) for k in ("bk_c", "bk_near", "bk_swa")]
    full = lambda a: pl.BlockSpec(a.shape, lambda h: (0, 0))
    per_head = lambda a: pl.BlockSpec((1,) + a.shape, lambda h: (h, 0, 0))
    return pl.pallas_call(
        _tables_kernel,
        name="bias_tables",
        grid=(NSA_HEADS,),
        in_specs=[pl.BlockSpec(memory_space=pltpu.SMEM)] + [full(a) for a in buckets],
        out_specs=[per_head(a) for a in buckets],
        out_shape=[jax.ShapeDtypeStruct((NSA_HEADS,) + a.shape, F32) for a in buckets],
        compiler_params=_params(("arbitrary",)),
    )(rel_bias, *buckets)


FFN_TM = 512
FFN_TF = 256


def _ffn_ln_kernel(x_ref, w1_ref, w2_ref, g_ref, b_ref, o_ref, ob_ref, act_ref):
    f = w2_ref.shape[0]
    x = x_ref[...]
    xb = x.astype(BF16)
    for c in range(f // FFN_TF):
        gt = _dot(xb, w1_ref[:, c * FFN_TF:(c + 1) * FFN_TF])
        up = _dot(xb, w1_ref[:, f + c * FFN_TF:f + (c + 1) * FFN_TF])
        act_ref[:, c * FFN_TF:(c + 1) * FFN_TF] = (gt * jax.nn.sigmoid(gt) * up).astype(BF16)
    y = DN_ALPHA * x + 0.5 * _dot(act_ref[...], w2_ref[...])
    out = _layer_norm(y, g_ref[...], b_ref[...])
    o_ref[...] = out
    ob_ref[...] = out.astype(BF16)


def _ffn_ln(x, w1, w2, g, b):
    n, d = x.shape
    f = w2.shape[0]
    tm = min(FFN_TM, n)
    resident = lambda a: pl.BlockSpec(a.shape, lambda i: (0, 0), pipeline_mode=pl.Buffered(1))
    return pl.pallas_call(
        _ffn_ln_kernel,
        name="ffn_ln",
        grid=(n // tm,),
        in_specs=[pl.BlockSpec((tm, d), lambda i: (i, 0)), resident(w1), resident(w2), resident(g), resident(b)],
        out_specs=[pl.BlockSpec((tm, d), lambda i: (i, 0)), pl.BlockSpec((tm, d), lambda i: (i, 0))],
        out_shape=[jax.ShapeDtypeStruct((n, d), F32), jax.ShapeDtypeStruct((n, d), BF16)],
        scratch_shapes=[pltpu.VMEM((tm, f), BF16)],
        compiler_params=_params(("arbitrary",)),
    )(x, w1, w2, g, b)


PROJ_TM = 512


def _proj_kernel(xb_ref, wft_ref, wk_ref, wc_ref, ws_ref, wst_ref, zt_ref, zk_ref, zc_ref, fz_ref, gt_ref):
    xb = xb_ref[...]
    zt_ref[:ROW_VSLC, :] = (_dot_nt(wft_ref[:ROW_VSLC, :], xb) * Q_SCALE).astype(BF16)
    zt_ref[ROW_VSLC:, :] = _dot_nt(wft_ref[ROW_VSLC:, :], xb).astype(BF16)
    zk_ref[...] = _dot(xb, wk_ref[...]).astype(BF16)
    zc_ref[...] = _dot(xb, wc_ref[...]).astype(BF16)
    fz_ref[...] = _dot(xb, ws_ref[...])
    gt_ref[...] = _dot_nt(wst_ref[...], xb)


def _project(xb, wft, wk, wc, ws, wst):
    n, d = xb.shape
    tm = min(PROJ_TM, n)
    res = lambda shp: pl.BlockSpec(shp, lambda i: (0, 0))
    return pl.pallas_call(
        _proj_kernel,
        name="proj_in",
        grid=(n // tm,),
        in_specs=[pl.BlockSpec((tm, d), lambda i: (i, 0)),
                  res(wft.shape), res(wk.shape), res(wc.shape), res(ws.shape), res(wst.shape)],
        out_specs=[pl.BlockSpec((ROWS_FEAT, tm), lambda i: (0, i)),
                   pl.BlockSpec((tm, COLS_K), lambda i: (i, 0)),
                   pl.BlockSpec((tm, 4 * HEAD_DIM), lambda i: (i, 0)),
                   pl.BlockSpec((tm, LANES), lambda i: (i, 0)),
                   pl.BlockSpec((ROWS_SMALL, tm), lambda i: (0, i))],
        out_shape=[jax.ShapeDtypeStruct((ROWS_FEAT, n), BF16),
                   jax.ShapeDtypeStruct((n, COLS_K), BF16),
                   jax.ShapeDtypeStruct((n, 4 * HEAD_DIM), BF16),
                   jax.ShapeDtypeStruct((n, LANES), F32),
                   jax.ShapeDtypeStruct((ROWS_SMALL, n), F32)],
        compiler_params=_params(("arbitrary",)),
    )(xb, wft, wk, wc, ws, wst)


def _gelu_tanh(x):
    return 0.5 * x * (1.0 + jnp.tanh(math.sqrt(2.0 / math.pi) * (x + 0.044715 * (x * x * x))))


def _compress_kernel(ck_ref, cv_ref, pek_ref, pev_ref, wk1_ref, wk2_ref, wv1_ref, wv2t_ref, kc_ref, vct_ref):
    half = CMP_STRIDE * HEAD_DIM
    nch = ck_ref.shape[2]

    def hidden(c_ref, pe_ref, w1_ref):
        c = c_ref[0, 0].astype(F32)
        top = (c + pe_ref[:, :half]).astype(BF16)
        bot = (c + pe_ref[:, half:]).astype(BF16)
        a = _dot(top, w1_ref[:half, :])
        bm = _dot(bot, w1_ref[half:, :])
        h = a + pltpu.roll(bm, nch - 1, axis=0)
        return _gelu_tanh(h).astype(BF16)

    kc = _dot(hidden(ck_ref, pek_ref, wk1_ref), wk2_ref[...])
    kc_ref[0, 0] = jnp.concatenate([kc, jnp.zeros_like(kc)], axis=1).astype(BF16)
    vct_ref[0, 0] = _dot_nt(wv2t_ref[...], hidden(cv_ref, pev_ref, wv1_ref)).astype(BF16)


def _compress(zc4, pek, pev, wk1, wk2, wv1, wv2t):
    bsz, _, nch, wide = zc4.shape
    res = lambda shp: pl.BlockSpec(shp, lambda b, g: (0, 0))
    return pl.pallas_call(
        _compress_kernel,
        name="nsa_compress",
        grid=(bsz, NSA_GROUPS),
        in_specs=[pl.BlockSpec((1, 1, nch, wide), lambda b, g: (b, g, 0, 0)),
                  pl.BlockSpec((1, 1, nch, wide), lambda b, g: (b, NSA_GROUPS + g, 0, 0)),
                  res(pek.shape), res(pev.shape), res(wk1.shape), res(wk2.shape), res(wv1.shape), res(wv2t.shape)],
        out_specs=[pl.BlockSpec((1, 1, nch, LANES), lambda b, g: (b, g, 0, 0)),
                   pl.BlockSpec((1, 1, HEAD_DIM, nch), lambda b, g: (b, g, 0, 0))],
        out_shape=[jax.ShapeDtypeStruct((bsz, NSA_GROUPS, nch, LANES), BF16),
                   jax.ShapeDtypeStruct((bsz, NSA_GROUPS, HEAD_DIM, nch), BF16)],
        compiler_params=_params(("arbitrary", "arbitrary")),
    )(zc4, zc4, pek, pev, wk1, wk2, wv1, wv2t)


SEL_TK = 256


def _with_sum_rows(vt):
    return jnp.concatenate([vt, jnp.ones((SUM_ROWS, vt.shape[1]), vt.dtype)], axis=0)


def _softmax_update(s, m, acc, vt):
    mn = jnp.maximum(m, jnp.max(s, axis=0, keepdims=True))
    p = jnp.exp2(s - mn).astype(BF16)
    acc = jnp.exp2(m - mn) * acc + _dot(_with_sum_rows(vt), p)
    return mn, acc


def _softmax_init(width):
    return jnp.full((1, width), -jnp.inf, F32), jnp.zeros((HEAD_DIM + SUM_ROWS, width), F32)


def _softmax_finish(acc):
    return acc[:HEAD_DIM] * (1.0 / acc[HEAD_DIM:HEAD_DIM + 1])


def _nsa_kernel(q_ref, gate_ref, kc_ref, vct_ref, ks_ref, vst_ref, kw_ref, vwt_ref, oh_ref, ovt_ref,
                u_ref, bn_ref, band_ref, o_ref, sa_ref, sb_ref, part_ref):
    g = pl.program_id(1)
    qb = pl.program_id(2)
    nch = kc_ref.shape[2]
    seq = ks_ref.shape[0]
    tq = NSA_TQ
    wide = HPG * tq

    def lanes4(fn):
        return jnp.concatenate([fn(hp) for hp in range(HPG)], axis=1)

    def gate(hp, branch):
        return jax.nn.sigmoid(gate_ref[pl.ds(g * (HPG * 3) + hp * 3 + branch, 1), :])

    qs = [q_ref[hp * HEAD_DIM:(hp + 1) * HEAD_DIM, :] for hp in range(HPG)]
    one_row = (lax.broadcasted_iota(jnp.int32, (HEAD_DIM, tq), 0) == 0).astype(BF16)
    q_pad = lanes4(lambda hp: jnp.concatenate([qs[hp], one_row], axis=0))
    key_bias_lane = (lax.broadcasted_iota(jnp.int32, (1, LANES), 1) == HEAD_DIM).astype(F32)
    bn_prev = lanes4(lambda hp: bn_ref[hp, 0:tq, :])
    bn_diag = lanes4(lambda hp: bn_ref[hp, tq:2 * tq, :])

    off = pl.multiple_of(nch - (tq // CMP_STRIDE) * qb, 8)
    s_c = _dot(kc_ref[0, 0], q_pad) + lanes4(lambda hp: u_ref[hp, pl.ds(off, nch), :])
    m_c = jnp.max(s_c, axis=0, keepdims=True)
    p_c = jnp.exp2(s_c - m_c)
    l_c = jnp.sum(p_c, axis=0, keepdims=True)
    p_c = p_c * jnp.where(m_c > NEG_TEST, 1.0 / jnp.maximum(l_c, 1e-30), 0.0)
    o_c = _dot(vct_ref[0, 0], p_c.astype(BF16))

    n_win = NSA_WINDOW // tq
    band4 = lanes4(lambda hp: band_ref[...])
    s_tiles, v_tiles = [], []
    for a in range(n_win + 1):
        ti = qb - n_win + a
        t0 = pl.multiple_of(jnp.maximum(ti, 0) * tq, tq)
        k_a = kw_ref[pl.ds(t0, tq), :]
        if a < n_win:
            k_a = k_a + (key_bias_lane * jnp.where(ti >= 0, 0.0, NEG)).astype(BF16)
        s_a = _dot(k_a, q_pad)
        if a == 0:
            s_a = s_a + band4
        if a == n_win - 1:
            s_a = s_a + bn_prev
        if a == n_win:
            s_a = s_a + bn_diag
        s_tiles.append(s_a)
        v_tiles.append(vwt_ref[:, pl.ds(t0, tq)])
    s_w = jnp.concatenate(s_tiles, axis=0)
    m_w = jnp.max(s_w, axis=0, keepdims=True)
    p_w = jnp.exp2(s_w - m_w).astype(BF16)
    o_w = _softmax_finish(_dot(_with_sum_rows(jnp.concatenate(v_tiles, axis=1)), p_w))
    for hp in range(HPG):
        sl = slice(hp * tq, (hp + 1) * tq)
        part_ref[hp * HEAD_DIM:(hp + 1) * HEAD_DIM, :] = gate(hp, 0) * o_c[:, sl] + gate(hp, 2) * o_w[:, sl]

    p_sum = p_c[:, 0:tq]
    for hp in range(1, HPG):
        p_sum = p_sum + p_c[:, hp * tq:(hp + 1) * tq]
    imp = jnp.dot(ovt_ref[...], p_sum, preferred_element_type=F32, precision=lax.Precision.HIGHEST)
    nsp = imp.shape[0]
    blk = lax.broadcasted_iota(jnp.int32, (nsp, tq), 0).astype(F32)
    r_lane = lax.broadcasted_iota(jnp.int32, (nsp, tq), 1)
    cur = ((tq // SEL_BLOCK) * qb + jnp.right_shift(r_lane, SEL_BLOCK.bit_length() - 1)).astype(F32)
    forced = (blk == 0.0) | (blk == cur) | (blk == cur - 1.0)
    valid = blk <= cur
    score = jnp.where(forced, jnp.inf, jnp.where(valid, imp, -jnp.inf))
    sel = jnp.zeros((nsp, tq), jnp.bool_)
    for _ in range(SEL_TOPK):
        mx = jnp.max(score, axis=0, keepdims=True)
        first = jnp.min(jnp.where(score == mx, blk, 1e9), axis=0, keepdims=True)
        hit = blk == first
        sel = sel | hit
        score = jnp.where(hit, -jnp.inf, score)
    sel = sel & valid
    selb = jnp.where(sel, 0.0, NEG)
    far_lim = ((qb - 1) * (tq // SEL_BLOCK)).astype(F32)
    selb_far = jnp.where(blk < far_lim, selb, NEG).astype(BF16)
    selb = selb.astype(BF16)
    q_near = lanes4(lambda hp: jnp.concatenate([qs[hp], selb], axis=0))
    q_far = lanes4(lambda hp: jnp.concatenate([qs[hp], selb_far], axis=0))

    p0 = pl.multiple_of(jnp.maximum(qb - 1, 0) * tq, tq)
    d0 = pl.multiple_of(qb * tq, tq)
    p0_ids = pl.multiple_of(jnp.where(qb > 0, p0, seq - tq), tq)
    kp = ks_ref[pl.ds(p0, tq), :] + oh_ref[pl.ds(p0_ids, tq), :]
    kd = ks_ref[pl.ds(d0, tq), :] + oh_ref[pl.ds(d0, tq), :]
    s_p = _dot(kp, q_near) + bn_prev
    s_d = _dot(kd, q_near) + bn_diag
    s_n = jnp.concatenate([s_p, s_d], axis=0)
    vt_n = jnp.concatenate([vst_ref[:, pl.ds(p0, tq)], vst_ref[:, pl.ds(d0, tq)]], axis=1)
    carry = _softmax_update(s_n, *_softmax_init(wide), vt_n)

    def far_scores(tile, s_ref):
        k0 = pl.multiple_of(jnp.minimum(tile * SEL_TK, seq - SEL_TK), SEL_TK)
        s_ref[...] = _dot(ks_ref[pl.ds(k0, SEL_TK), :] + oh_ref[pl.ds(k0, SEL_TK), :], q_far)

    def far_consume(s_ref, tile, carry):
        k0 = pl.multiple_of(tile * SEL_TK, SEL_TK)
        return _softmax_update(s_ref[...], *carry, vst_ref[:, pl.ds(k0, SEL_TK)])

    def far_body(j, carry):
        far_scores(2 * j + 1, sb_ref)
        carry = far_consume(sa_ref, 2 * j, carry)
        far_scores(2 * j + 2, sa_ref)
        return far_consume(sb_ref, 2 * j + 1, carry)

    far_tiles = 2 * SEL_TK // tq
    n_far = (jnp.maximum(qb - 1, 0) + far_tiles - 1) // far_tiles
    far_scores(0, sa_ref)
    _, acc_s = lax.fori_loop(0, n_far, far_body, carry)
    o_s = _softmax_finish(acc_s)

    for hp in range(HPG):
        rows = slice(hp * HEAD_DIM, (hp + 1) * HEAD_DIM)
        o_ref[rows, :] = (part_ref[rows, :] + gate(hp, 1) * o_s[:, hp * tq:(hp + 1) * tq]).astype(BF16)


def _nsa(zt, zk, gt, kc, vct, onehot, ovt, u_tab, bn_tab, band0, bsz, seq):
    nq = seq // NSA_TQ
    nch = seq // CMP_STRIDE
    gw = HPG * HEAD_DIM
    n = bsz * seq
    const2 = lambda shp: pl.BlockSpec(shp, lambda b, g, q: (0, 0))
    return pl.pallas_call(
        _nsa_kernel,
        name="nsa_attn",
        grid=(bsz, NSA_GROUPS, nq),
        in_specs=[pl.BlockSpec((gw, NSA_TQ), lambda b, g, q: (ROW_QA // gw + g, b * nq + q)),
                  pl.BlockSpec((ROWS_SMALL, NSA_TQ), lambda b, g, q: (0, b * nq + q)),
                  pl.BlockSpec((1, 1, nch, LANES), lambda b, g, q: (b, g, 0, 0)),
                  pl.BlockSpec((1, 1, HEAD_DIM, nch), lambda b, g, q: (b, g, 0, 0)),
                  pl.BlockSpec((seq, LANES), lambda b, g, q: (b, CB_KSLC + g)),
                  pl.BlockSpec((HEAD_DIM, seq), lambda b, g, q: (ROW_VSLC // HEAD_DIM + g, b)),
                  pl.BlockSpec((seq, LANES), lambda b, g, q: (b, CB_KWIN + g)),
                  pl.BlockSpec((HEAD_DIM, seq), lambda b, g, q: (ROW_VWIN // HEAD_DIM + g, b)),
                  const2(onehot.shape), const2(ovt.shape),
                  pl.BlockSpec((HPG,) + u_tab.shape[1:], lambda b, g, q: (g, 0, 0)),
                  pl.BlockSpec((HPG,) + bn_tab.shape[1:], lambda b, g, q: (g, 0, 0)),
                  const2(band0.shape)],
        out_specs=pl.BlockSpec((gw, NSA_TQ), lambda b, g, q: (g, b * nq + q)),
        out_shape=jax.ShapeDtypeStruct((NSA_HEADS * HEAD_DIM, n), BF16),
        scratch_shapes=[pltpu.VMEM((SEL_TK, HPG * NSA_TQ), F32), pltpu.VMEM((SEL_TK, HPG * NSA_TQ), F32),
                        pltpu.VMEM((gw, NSA_TQ), F32)],
        compiler_params=_params(("arbitrary", "arbitrary", "arbitrary")),
    )(zt, gt, kc, vct, zk, zt, zk, zt, onehot, ovt, u_tab, bn_tab, band0)


SWA_SUB = 4


def _swa_kernel(sink_ref, q_ref, kcur_ref, kprev_ref, vcur_ref, vprev_ref, bs_ref, o_ref):
    g = pl.program_id(1)
    t = pl.program_id(2)
    tq = SWA_WINDOW
    zeros_q = jnp.zeros((HEAD_DIM, tq), BF16)

    def lanes4(fn):
        return jnp.concatenate([fn(hp) for hp in range(HPG)], axis=1)

    bs_prev = lanes4(lambda hp: bs_ref[hp, 0:tq, :])
    bs_diag = lanes4(lambda hp: bs_ref[hp, tq:2 * tq, :])
    sink = lanes4(lambda hp: jnp.full((1, tq), sink_ref[g * HPG + hp] * LOG2E, F32))
    for sub in range(SWA_SUB):
        cs = slice(sub * tq, (sub + 1) * tq)
        q_pad = lanes4(lambda hp: jnp.concatenate(
            [q_ref[hp * HEAD_DIM:(hp + 1) * HEAD_DIM, cs], zeros_q], axis=0))
        if sub == 0:
            kp, vp = kprev_ref[...], vprev_ref[...]
        else:
            ps = slice((sub - 1) * tq, sub * tq)
            kp, vp = kcur_ref[ps, :], vcur_ref[:, ps]
        s_p = _dot(kp, q_pad) + bs_prev
        if sub == 0:
            s_p = jnp.where(t > 0, s_p, NEG)
        s_d = _dot(kcur_ref[cs, :], q_pad) + bs_diag
        s = jnp.concatenate([s_p, s_d], axis=0)
        m = jnp.maximum(jnp.max(s, axis=0, keepdims=True), sink)
        p = jnp.exp2(s - m).astype(BF16)
        acc = _dot(_with_sum_rows(jnp.concatenate([vp, vcur_ref[:, cs]], axis=1)), p)
        o = acc[:HEAD_DIM] * (1.0 / (acc[HEAD_DIM:HEAD_DIM + 1] + jnp.exp2(sink - m)))
        for hp in range(HPG):
            o_ref[hp * HEAD_DIM:(hp + 1) * HEAD_DIM, cs] = o[:, hp * tq:(hp + 1) * tq].astype(BF16)


def _swa(zt, zk, sinks, bs_tab, bsz, seq):
    tq = SWA_WINDOW
    nq = seq // tq
    nt = nq // SWA_SUB
    wide = SWA_SUB * tq
    gw = HPG * HEAD_DIM
    n = bsz * seq
    prev = lambda b, t: b * nq + jnp.maximum(t * SWA_SUB - 1, 0)
    return pl.pallas_call(
        _swa_kernel,
        name="swa_attn",
        grid=(bsz, SWA_GROUPS, nt),
        in_specs=[pl.BlockSpec(memory_space=pltpu.SMEM),
                  pl.BlockSpec((gw, wide), lambda b, g, t: (ROW_QB // gw + g, b * nt + t)),
                  pl.BlockSpec((wide, LANES), lambda b, g, t: (b * nt + t, CB_KB + g)),
                  pl.BlockSpec((tq, LANES), lambda b, g, t: (prev(b, t), CB_KB + g)),
                  pl.BlockSpec((HEAD_DIM, wide), lambda b, g, t: (ROW_VB // HEAD_DIM + g, b * nt + t)),
                  pl.BlockSpec((HEAD_DIM, tq), lambda b, g, t: (ROW_VB // HEAD_DIM + g, prev(b, t))),
                  pl.BlockSpec((HPG,) + bs_tab.shape[1:], lambda b, g, t: (g, 0, 0))],
        out_specs=pl.BlockSpec((gw, wide), lambda b, g, t: (g, b * nt + t)),
        out_shape=jax.ShapeDtypeStruct((SWA_HEADS * HEAD_DIM, n), BF16),
        compiler_params=_params(("arbitrary", "arbitrary", "arbitrary")),
    )(sinks, zt, zk, zk, zt, zt, bs_tab)


def _fox_keys_kernel(fz_ref, bf_ref, k_ref, o_ref, c_ref):
    seq = fz_ref.shape[0]
    ch = LANES
    logf = jax.nn.log_sigmoid(fz_ref[...] + bf_ref[...])
    tri = (lax.broadcasted_iota(jnp.int32, (ch, ch), 0) >= lax.broadcasted_iota(jnp.int32, (ch, ch), 1)).astype(F32)
    carry = jnp.zeros((1, LANES), F32)
    for i in range(seq // ch):
        c = jnp.dot(tri, logf[i * ch:(i + 1) * ch, :], preferred_element_type=F32,
                    precision=lax.Precision.HIGHEST) + carry
        c_ref[i * ch:(i + 1) * ch, :] = c
        carry = c[ch - 1:ch, :]
    neg = c_ref[...] * (-LOG2E)
    hi = neg.astype(BF16)
    r1 = neg - hi.astype(F32)
    mid = r1.astype(BF16)
    lo = (r1 - mid.astype(F32)).astype(BF16)
    parts = jnp.concatenate([hi, mid, lo], axis=1)
    row = lax.broadcasted_iota(jnp.int32, (3 * LANES, LANES), 0)
    col = lax.broadcasted_iota(jnp.int32, (3 * LANES, LANES), 1)
    lane = lax.broadcasted_iota(jnp.int32, (seq, LANES), 1)
    for h in range(FOX_HEADS):
        src = COL_FC + h
        place = ((row == src) & (col == LANE_C)) | ((row == LANES + src) & (col == LANE_C + 1)) \
            | ((row == 2 * LANES + src) & (col == LANE_C + 2))
        bias = _dot(parts, place.astype(BF16)).astype(BF16)
        hs = slice(h * LANES, (h + 1) * LANES)
        o_ref[:, hs] = jnp.where(lane < HEAD_DIM, k_ref[:, hs], bias)


def _fox_keys(fz, bf_row, zk, bsz, seq):
    wide = FOX_HEADS * LANES
    return pl.pallas_call(
        _fox_keys_kernel,
        name="fox_keys",
        grid=(bsz,),
        in_specs=[pl.BlockSpec((seq, LANES), lambda b: (b, 0)),
                  pl.BlockSpec((1, LANES), lambda b: (0, 0)),
                  pl.BlockSpec((seq, wide), lambda b: (b, CB_KC))],
        out_specs=pl.BlockSpec((seq, wide), lambda b: (b, 0)),
        out_shape=jax.ShapeDtypeStruct((bsz * seq, wide), BF16),
        scratch_shapes=[pltpu.VMEM((seq, LANES), F32)],
        compiler_params=_params(("arbitrary",)),
    )(fz, bf_row, zk)


FOX_T = 512
FOX_HPS = 2


def _fox_kernel(q_ref, k_ref, vt_ref, o_ref, *s_refs):
    qi = pl.program_id(2)
    t = q_ref.shape[1]
    tk = t // 2
    ones_rows = (lax.broadcasted_iota(jnp.int32, (HEAD_DIM, t), 0) < 3).astype(BF16)
    rows = lambda hh: slice(hh * HEAD_DIM, (hh + 1) * HEAD_DIM)
    q_aug = [jnp.concatenate([q_ref[rows(hh), :], ones_rows], axis=0) for hh in range(FOX_HPS)]

    def scores(tile, slot):
        k0 = pl.multiple_of(tile * tk, tk)
        for hh in range(FOX_HPS):
            s_refs[2 * hh + slot][...] = _dot(k_ref[pl.ds(k0, tk), hh * LANES:(hh + 1) * LANES], q_aug[hh])

    def consume(slot, tile, carry, first_masked_query=None):
        k0 = pl.multiple_of(tile * tk, tk)
        out = []
        for hh in range(FOX_HPS):
            s = s_refs[2 * hh + slot][...]
            if first_masked_query is not None:
                key = lax.broadcasted_iota(jnp.int32, (tk, t), 0) + first_masked_query
                s = jnp.where(key > lax.broadcasted_iota(jnp.int32, (tk, t), 1), NEG, s)
            out.append(_softmax_update(s, *carry[hh], vt_ref[rows(hh), pl.ds(k0, tk)]))
        return tuple(out)

    def body(j, carry):
        scores(2 * j + 1, 1)
        carry = consume(0, 2 * j, carry)
        scores(2 * j + 2, 0)
        return consume(1, 2 * j + 1, carry)

    scores(0, 0)
    carry = lax.fori_loop(0, qi, body, tuple(_softmax_init(t) for _ in range(FOX_HPS)))
    scores(2 * qi + 1, 1)
    carry = consume(0, 2 * qi, carry, 0)
    carry = consume(1, 2 * qi + 1, carry, tk)
    for hh in range(FOX_HPS):
        o_ref[rows(hh), :] = _softmax_finish(carry[hh][1]).astype(BF16)


def _fox(zt, kaug, bsz, seq):
    t = min(FOX_T, seq)
    nq = seq // t
    n = bsz * seq
    hw = FOX_HPS * HEAD_DIM
    return pl.pallas_call(
        _fox_kernel,
        name="fox_attn",
        grid=(bsz, FOX_HEADS // FOX_HPS, nq),
        in_specs=[pl.BlockSpec((hw, t), lambda b, h, q: (ROW_QC // hw + h, b * nq + q)),
                  pl.BlockSpec((seq, FOX_HPS * LANES), lambda b, h, q: (b, h)),
                  pl.BlockSpec((hw, seq), lambda b, h, q: (ROW_VC // hw + h, b))],
        out_specs=pl.BlockSpec((hw, t), lambda b, h, q: (h, b * nq + q)),
        out_shape=jax.ShapeDtypeStruct((FOX_HEADS * HEAD_DIM, n), BF16),
        scratch_shapes=[pltpu.VMEM((t // 2, t), F32) for _ in range(2 * FOX_HPS)],
        compiler_params=_params(("arbitrary", "arbitrary", "arbitrary")),
    )(zt, kaug, zt)


MERGE_TM = 512


def _merge_ln_kernel(x_ref, xb_ref, oa_ref, ob_ref, oc_ref, wg_ref, bg_ref, wbr_ref, wo_ref, g_ref, b_ref,
                     o_ref, obf_ref):
    d = x_ref.shape[1]
    xb = xb_ref[...]
    merged = None
    for i, br_ref in enumerate((oa_ref, ob_ref, oc_ref)):
        gate = jax.nn.sigmoid(_dot(xb, wg_ref[:, i * d:(i + 1) * d]) + bg_ref[:, i * d:(i + 1) * d])
        term = gate * _dot_tn(br_ref[...], wbr_ref[i])
        merged = term if merged is None else merged + term
    mix = _dot(merged.astype(BF16), wo_ref[...])
    out = _layer_norm(DN_ALPHA * x_ref[...] + mix, g_ref[...], b_ref[...])
    o_ref[...] = out
    obf_ref[...] = out.astype(BF16)


def _merge_ln(x, xb, ota, otb, otc, wg, bg, wbr, wo, g, b):
    n, d = x.shape
    tm = min(MERGE_TM, n)
    w = ota.shape[0]
    tok = pl.BlockSpec((tm, d), lambda i: (i, 0))
    feat = pl.BlockSpec((w, tm), lambda i: (0, i))
    res2 = lambda shp: pl.BlockSpec(shp, lambda i: (0, 0))
    return pl.pallas_call(
        _merge_ln_kernel,
        name="merge_ln",
        grid=(n // tm,),
        in_specs=[tok, tok, feat, feat, feat, res2(wg.shape), res2(bg.shape),
                  pl.BlockSpec(wbr.shape, lambda i: (0, 0, 0)), res2(wo.shape), res2(g.shape), res2(b.shape)],
        out_specs=[tok, tok],
        out_shape=[jax.ShapeDtypeStruct((n, d), F32), jax.ShapeDtypeStruct((n, d), BF16)],
        compiler_params=_params(("arbitrary",)),
    )(x, xb, ota, otb, otc, wg, bg, wbr, wo, g, b)


def _split_w_in(w_in):
    sizes = [NSA_HEADS * HEAD_DIM, 6 * NSA_GROUPS * HEAD_DIM, N_GA, SWA_HEADS * HEAD_DIM,
             SWA_GROUPS * HEAD_DIM, SWA_GROUPS * HEAD_DIM, FOX_HEADS * HEAD_DIM, FOX_HEADS * HEAD_DIM,
             FOX_HEADS * HEAD_DIM, FOX_HEADS]
    q_a, kv_a, g_a, q_b, k_b, v_b, q_c, k_c, v_c, f_c = jnp.split(w_in, np.cumsum(sizes)[:-1].tolist(), axis=-1)
    k_cmp, v_cmp, k_slc, v_slc, k_win, v_win = jnp.split(kv_a, 6, axis=-1)
    depth, d = w_in.shape[0], w_in.shape[1]

    def pad_heads(w):
        nh = w.shape[-1] // HEAD_DIM
        w = w.reshape(depth, d, nh, HEAD_DIM)
        return jnp.concatenate([w, jnp.zeros_like(w)], axis=-1).reshape(depth, d, nh * LANES)

    w_feat_t = jnp.swapaxes(jnp.concatenate([q_a, q_b, q_c, v_slc, v_win, v_b, v_c], axis=-1), 1, 2)
    w_k = jnp.concatenate([pad_heads(k_c), pad_heads(k_slc), pad_heads(k_win), pad_heads(k_b)], axis=-1)
    w_cmp = jnp.concatenate([k_cmp, v_cmp], axis=-1)
    small = jnp.concatenate([g_a, f_c], axis=-1)
    w_small = jnp.concatenate([small, jnp.zeros((depth, d, LANES - ROWS_SMALL), small.dtype)], axis=-1)
    w_small_t = jnp.swapaxes(small, 1, 2)
    return w_feat_t, w_k, w_cmp, w_small, w_small_t


def kernel(x, rel_bias, ln1_g, ln1_b, ffn1_w1, ffn1_w2, w_in, cmp_pe_k, cmp_pe_v, cmp_k_w1, cmp_k_w2, cmp_v_w1,
           cmp_v_w2, swa_sinks, fox_b_f, w_br_a, w_br_b, w_br_c, w_gate, b_gate, w_out, ln2_g, ln2_b, ffn2_w1,
           ffn2_w2, ln3_g, ln3_b):
    bsz, seq, d = x.shape
    n = bsz * seq
    depth = w_in.shape[0]
    nch = seq // CMP_STRIDE
    tabs = _static_tables(seq)
    onehot = jnp.asarray(tabs["onehot"], BF16)
    ovt = jnp.asarray(tabs["ovT"])
    band0 = jnp.asarray(tabs["band0"])
    u_tab, bn_tab, bs_tab = _bias_tables(rel_bias, tabs)

    bf = lambda w: w.astype(BF16)
    w_feat_t, w_k, w_cmp, w_small, w_small_t = [bf(w) for w in _split_w_in(w_in)]
    f1w1, f1w2, f2w1, f2w2 = bf(ffn1_w1), bf(ffn1_w2), bf(ffn2_w1), bf(ffn2_w2)
    ck1, ck2, cv1 = bf(cmp_k_w1), bf(cmp_k_w2), bf(cmp_v_w1)
    cv2t = bf(jnp.swapaxes(cmp_v_w2, 1, 2))
    pek = cmp_pe_k.reshape(depth, 1, CMP_BLOCK * HEAD_DIM)
    pev = cmp_pe_v.reshape(depth, 1, CMP_BLOCK * HEAD_DIM)
    wg, wo = bf(w_gate), bf(w_out)
    wbr = bf(jnp.stack([w_br_a, w_br_b, w_br_c], axis=1))
    bf_rows = jnp.zeros((depth, 1, LANES), F32).at[:, 0, COL_FC:COL_FC + FOX_HEADS].set(fox_b_f)
    row = lambda v: v.reshape(1, -1)

    h = x.reshape(n, d)
    for l in range(depth):
        h, hb = _ffn_ln(h, f1w1[l], f1w2[l], row(ln1_g[l]), row(ln1_b[l]))
        zt, zk, zc, fz, gt = _project(hb, w_feat_t[l], w_k[l], w_cmp[l], w_small[l], w_small_t[l])
        zc4 = zc.reshape(bsz, nch, CMP_STRIDE, 2 * NSA_GROUPS, HEAD_DIM).transpose(0, 3, 1, 2, 4)
        zc4 = zc4.reshape(bsz, 2 * NSA_GROUPS, nch, CMP_STRIDE * HEAD_DIM)
        kc, vct = _compress(zc4, pek[l], pev[l], ck1[l], ck2[l], cv1[l], cv2t[l])
        ot_a = _nsa(zt, zk, gt, kc, vct, onehot, ovt, u_tab, bn_tab, band0, bsz, seq)
        ot_b = _swa(zt, zk, swa_sinks[l], bs_tab, bsz, seq)
        kaug = _fox_keys(fz, bf_rows[l], zk, bsz, seq)
        ot_c = _fox(zt, kaug, bsz, seq)
        h, hb = _merge_ln(h, hb, ot_a, ot_b, ot_c, wg[l], row(b_gate[l]), wbr[l], wo[l],
                          row(ln2_g[l]), row(ln2_b[l]))
        h, hb = _ffn_ln(h, f2w1[l], f2w2[l], row(ln3_g[l]), row(ln3_b[l]))
    return h.reshape(bsz, seq, d)
```

```python
import functools
import math

import numpy as np
import jax
import jax.numpy as jnp
from jax import lax
from jax.experimental import pallas as pl
from jax.experimental.pallas import tpu as pltpu

F32 = jnp.float32
BF16 = jnp.bfloat16

D_MODEL = 1024
DEPTH = 4
HEAD_DIM = 64
NSA_HEADS = 8
NSA_GROUPS = 2
HPG = NSA_HEADS // NSA_GROUPS
CMP_BLOCK = 32
CMP_STRIDE = 16
CMP_HIDDEN = 256
SEL_BLOCK = 64
SEL_TOPK = 8
NSA_WINDOW = 512
SWA_HEADS = 8
SWA_GROUPS = 2
SWA_WINDOW = 128
FOX_HEADS = 8
NSA_TQ = 256
NSA_SEL_TQ = 512
REL_BUCKETS = 32
REL_MAX_DIST = 128
D_FF = 2816
LN_EPS = 1e-5
DN_ALPHA = (2 * DEPTH) ** 0.25
SCALE = HEAD_DIM ** -0.5
LOG2E = math.log2(math.e)
Q_SCALE = SCALE * LOG2E
SUM_ROWS = 16

NEG = -1e30
NEG_TEST = -1e29
LANES = 128
VMEM_LIMIT = 56 * 1024 * 1024

ROW_QA, ROW_QB, ROW_QC = 0, 512, 1024
ROW_VSLC, ROW_VWIN, ROW_VB, ROW_VC = 1536, 1664, 1792, 1920
ROWS_FEAT = 2432
CB_KC, CB_KSLC, CB_KWIN, CB_KB = 0, 8, 10, 12
COLS_K = 14 * LANES
N_GA = 3 * NSA_HEADS
COL_FC = N_GA
ROWS_SMALL = 32
LANE_C = HEAD_DIM


def _params(sem, vmem=VMEM_LIMIT):
    return pltpu.CompilerParams(dimension_semantics=sem, vmem_limit_bytes=vmem)


def _layer_spec(w, layer, single_buffer=False):
    zeros = (0,) * (w.ndim - 1)
    mode = dict(pipeline_mode=pl.Buffered(1)) if single_buffer else {}
    return pl.BlockSpec((None,) + w.shape[1:], lambda *_: (layer,) + zeros, **mode)


def _dot(a, b):
    return jnp.dot(a, b, preferred_element_type=F32)


def _dot_nt(a, b):
    return lax.dot_general(a, b, (((1,), (1,)), ((), ())), preferred_element_type=F32)


def _dot_tn(a, b):
    return lax.dot_general(a, b, (((0,), (0,)), ((), ())), preferred_element_type=F32)


def _layer_norm(y, g, b):
    mu = jnp.mean(y, axis=-1, keepdims=True)
    yc = y - mu
    var = jnp.mean(yc * yc, axis=-1, keepdims=True)
    return yc * lax.rsqrt(var + LN_EPS) * g + b


def _bucket_np(n):
    n = np.maximum(n, 0)
    exact = REL_BUCKETS // 2
    ratio = np.log(np.maximum(n, 1).astype(np.float64) / exact) / math.log(REL_MAX_DIST / exact)
    large = exact + (ratio * (REL_BUCKETS - exact)).astype(np.int64)
    return np.where(n < exact, n, np.minimum(large, REL_BUCKETS - 1)).astype(np.int32)


def _static_tables(seq):
    nch = seq // CMP_STRIDE
    ns = seq // SEL_BLOCK

    def near_buckets(tq):
        r = np.arange(tq)[None, :]
        c = np.arange(tq)[:, None]
        d_prev = tq + r - c
        d_diag = r - c
        return d_prev, _bucket_np(d_prev), np.where(d_diag >= 0, _bucket_np(d_diag), -1)

    r = np.arange(NSA_TQ)[None, :]
    r_sel = np.arange(min(NSA_SEL_TQ, seq))[None, :]
    u = np.arange(2 * nch)[:, None]
    n_c = r_sel - (CMP_BLOCK - 1) - CMP_STRIDE * (u - nch)
    bk_c = np.where(n_c >= 0, _bucket_np(n_c), -1).astype(np.int32)
    _, bk_prev, bk_diag = near_buckets(NSA_TQ)
    bk_near = np.concatenate([bk_prev, bk_diag], axis=0).astype(np.int32)
    d_prev, bk_prev, bk_diag = near_buckets(SWA_WINDOW)
    bk_swa = np.concatenate([np.where(d_prev < SWA_WINDOW, bk_prev, -1), bk_diag], axis=0).astype(np.int32)
    band0 = np.where(np.arange(NSA_TQ)[:, None] > r, 0.0, NEG).astype(np.float32)
    s = np.arange(seq)[:, None]
    lane = np.arange(LANES)[None, :]
    onehot = (lane == HEAD_DIM + s // SEL_BLOCK).astype(np.float32)
    return dict(bk_c=bk_c, bk_near=bk_near, bk_swa=bk_swa, band0=band0, onehot=onehot)


def _tables_kernel(rel_ref, bkc_ref, bkn_ref, bks_ref, u_ref, bn_ref, bs_ref):
    h = pl.program_id(0)

    def build(bk, col, delta):
        base = rel_ref[REL_BUCKETS - 1, col] if delta else 0.0
        acc = jnp.zeros(bk.shape, F32)
        for b in range(REL_BUCKETS):
            acc = jnp.where(bk == b, (rel_ref[b, col] - base) * LOG2E, acc)
        return jnp.where(bk < 0, NEG, acc)

    u_ref[0] = build(bkc_ref[...], h, True)
    bn_ref[0] = build(bkn_ref[...], h, True)
    bs_ref[0] = build(bks_ref[...], NSA_HEADS + h, False)


def _bias_tables(rel_bias, tabs):
    buckets = [jnp.asarray(tabs[k]) for k in ("bk_c", "bk_near", "bk_swa")]
    full = lambda a: pl.BlockSpec(a.shape, lambda h: (0, 0))
    per_head = lambda a: pl.BlockSpec((1,) + a.shape, lambda h: (h, 0, 0))
    return pl.pallas_call(
        _tables_kernel,
        name="bias_tables",
        grid=(NSA_HEADS,),
        in_specs=[pl.BlockSpec(memory_space=pltpu.SMEM)] + [full(a) for a in buckets],
        out_specs=[per_head(a) for a in buckets],
        out_shape=[jax.ShapeDtypeStruct((NSA_HEADS,) + a.shape, F32) for a in buckets],
        compiler_params=_params(("arbitrary",)),
    )(rel_bias, *buckets)


FFN_TM = 512
FFN_TF = 256


def _ffn_ln_kernel(x_ref, w1_ref, w2_ref, g_ref, b_ref, o_ref, ob_ref, act_ref):
    f = w2_ref.shape[0]
    x = x_ref[...]
    xb = x.astype(BF16)
    for c in range(f // FFN_TF):
        gt = _dot(xb, w1_ref[:, c * FFN_TF:(c + 1) * FFN_TF])
        up = _dot(xb, w1_ref[:, f + c * FFN_TF:f + (c + 1) * FFN_TF])
        act_ref[:, c * FFN_TF:(c + 1) * FFN_TF] = (gt * jax.nn.sigmoid(gt) * up).astype(BF16)
    y = DN_ALPHA * x + 0.5 * _dot(act_ref[...], w2_ref[...])
    out = _layer_norm(y, g_ref[...], b_ref[...])
    o_ref[...] = out
    ob_ref[...] = out.astype(BF16)


def _ffn_ln(x, layer, w1, w2, g, b):
    n, d = x.shape
    f = w2.shape[1]
    tm = min(FFN_TM, n)
    resident = lambda a: _layer_spec(a, layer, single_buffer=True)
    return pl.pallas_call(
        _ffn_ln_kernel,
        name="ffn_ln",
        grid=(n // tm,),
        in_specs=[pl.BlockSpec((tm, d), lambda i: (i, 0)), resident(w1), resident(w2), resident(g), resident(b)],
        out_specs=[pl.BlockSpec((tm, d), lambda i: (i, 0)), pl.BlockSpec((tm, d), lambda i: (i, 0))],
        out_shape=[jax.ShapeDtypeStruct((n, d), F32), jax.ShapeDtypeStruct((n, d), BF16)],
        scratch_shapes=[pltpu.VMEM((tm, f), BF16)],
        compiler_params=_params(("arbitrary",)),
    )(x, w1, w2, g, b)


PROJ_TM = 512


def _proj_kernel(xb_ref, wft_ref, wk_ref, wc_ref, ws_ref, wst_ref, zt_ref, zk_ref, zc_ref, fz_ref, gt_ref):
    xb = xb_ref[...]
    zt_ref[:ROW_VSLC, :] = (_dot_nt(wft_ref[:ROW_VSLC, :], xb) * Q_SCALE).astype(BF16)
    zt_ref[ROW_VSLC:, :] = _dot_nt(wft_ref[ROW_VSLC:, :], xb).astype(BF16)
    zk_ref[...] = _dot(xb, wk_ref[...]).astype(BF16)
    zc_ref[...] = _dot(xb, wc_ref[...]).astype(BF16)
    fz_ref[...] = _dot(xb, ws_ref[...])
    gt_ref[...] = _dot_nt(wst_ref[...], xb)


def _project(xb, layer, wft, wk, wc, ws, wst):
    n, d = xb.shape
    tm = min(PROJ_TM, n)
    return pl.pallas_call(
        _proj_kernel,
        name="proj_in",
        grid=(n // tm,),
        in_specs=[pl.BlockSpec((tm, d), lambda i: (i, 0))] + [_layer_spec(w, layer) for w in (wft, wk, wc, ws, wst)],
        out_specs=[pl.BlockSpec((ROWS_FEAT, tm), lambda i: (0, i)),
                   pl.BlockSpec((tm, COLS_K), lambda i: (i, 0)),
                   pl.BlockSpec((tm, 4 * HEAD_DIM), lambda i: (i, 0)),
                   pl.BlockSpec((tm, LANES), lambda i: (i, 0)),
                   pl.BlockSpec((ROWS_SMALL, tm), lambda i: (0, i))],
        out_shape=[jax.ShapeDtypeStruct((ROWS_FEAT, n), BF16),
                   jax.ShapeDtypeStruct((n, COLS_K), BF16),
                   jax.ShapeDtypeStruct((n, 4 * HEAD_DIM), BF16),
                   jax.ShapeDtypeStruct((n, LANES), F32),
                   jax.ShapeDtypeStruct((ROWS_SMALL, n), F32)],
        compiler_params=_params(("arbitrary",)),
    )(xb, wft, wk, wc, ws, wst)


def _gelu_tanh(x):
    return 0.5 * x * (1.0 + jnp.tanh(math.sqrt(2.0 / math.pi) * (x + 0.044715 * (x * x * x))))


def _compress_kernel(ck_ref, cv_ref, pek_ref, pev_ref, wk1_ref, wk2_ref, wv1_ref, wv2t_ref, kc_ref, vct_ref):
    half = CMP_STRIDE * HEAD_DIM
    nch = ck_ref.shape[2]

    def hidden(c_ref, pe_ref, w1_ref):
        c = c_ref[0, 0].astype(F32)
        top = (c + pe_ref[:, :half]).astype(BF16)
        bot = (c + pe_ref[:, half:]).astype(BF16)
        a = _dot(top, w1_ref[:half, :])
        bm = _dot(bot, w1_ref[half:, :])
        h = a + pltpu.roll(bm, nch - 1, axis=0)
        return _gelu_tanh(h).astype(BF16)

    kc = _dot(hidden(ck_ref, pek_ref, wk1_ref), wk2_ref[...])
    kc_ref[0, 0] = jnp.concatenate([kc, jnp.zeros_like(kc)], axis=1).astype(BF16)
    vct_ref[0, 0] = _dot_nt(wv2t_ref[...], hidden(cv_ref, pev_ref, wv1_ref)).astype(BF16)


def _compress(zc4, layer, pek, pev, wk1, wk2, wv1, wv2t):
    bsz, _, nch, wide = zc4.shape
    return pl.pallas_call(
        _compress_kernel,
        name="nsa_compress",
        grid=(bsz, NSA_GROUPS),
        in_specs=[pl.BlockSpec((1, 1, nch, wide), lambda b, g: (b, g, 0, 0)),
                  pl.BlockSpec((1, 1, nch, wide), lambda b, g: (b, NSA_GROUPS + g, 0, 0))]
        + [_layer_spec(w, layer) for w in (pek, pev, wk1, wk2, wv1, wv2t)],
        out_specs=[pl.BlockSpec((1, 1, nch, LANES), lambda b, g: (b, g, 0, 0)),
                   pl.BlockSpec((1, 1, HEAD_DIM, nch), lambda b, g: (b, g, 0, 0))],
        out_shape=[jax.ShapeDtypeStruct((bsz, NSA_GROUPS, nch, LANES), BF16),
                   jax.ShapeDtypeStruct((bsz, NSA_GROUPS, HEAD_DIM, nch), BF16)],
        compiler_params=_params(("arbitrary", "arbitrary")),
    )(zc4, zc4, pek, pev, wk1, wk2, wv1, wv2t)


SEL_TK = 256


def _with_sum_rows(vt):
    return jnp.concatenate([vt, jnp.ones((SUM_ROWS, vt.shape[1]), vt.dtype)], axis=0)


def _softmax_update(s, m, acc, vt):
    mn = jnp.maximum(m, jnp.max(s, axis=0, keepdims=True))
    p = jnp.exp2(s - mn).astype(BF16)
    acc = jnp.exp2(m - mn) * acc + _dot(_with_sum_rows(vt), p)
    return mn, acc


def _softmax_init(width):
    return jnp.full((1, width), -jnp.inf, F32), jnp.zeros((HEAD_DIM + SUM_ROWS, width), F32)


def _softmax_finish(acc):
    return acc[:HEAD_DIM] * (1.0 / acc[HEAD_DIM:HEAD_DIM + 1])


def _gate_row(gate_ref, g, hp, branch):
    return jax.nn.sigmoid(gate_ref[pl.ds(g * (HPG * 3) + hp * 3 + branch, 1), :])


def _nsa_select_kernel(q_ref, gate_ref, kc_ref, vct_ref, u_ref, selb_ref, part_ref, sc_ref, psum_ref):
    g = pl.program_id(1)
    qb = pl.program_id(2)
    nch = kc_ref.shape[2]
    tq = q_ref.shape[1]

    def lanes4(fn):
        return jnp.concatenate([fn(hp) for hp in range(HPG)], axis=1)

    zeros_q = jnp.zeros((HEAD_DIM, tq), BF16)
    q_pad = lanes4(lambda hp: jnp.concatenate([q_ref[hp * HEAD_DIM:(hp + 1) * HEAD_DIM, :], zeros_q], axis=0))

    off = pl.multiple_of(nch - (tq // CMP_STRIDE) * qb, 8)
    sc_ref[...] = _dot(kc_ref[0, 0], q_pad) + lanes4(lambda hp: u_ref[hp, pl.ds(off, nch), :])
    s_c = sc_ref[...]
    m_c = jnp.max(s_c, axis=0, keepdims=True)
    p_c = jnp.exp2(s_c - m_c)
    l_c = jnp.sum(p_c, axis=0, keepdims=True)
    p_c = p_c * jnp.where(m_c > NEG_TEST, 1.0 / jnp.maximum(l_c, 1e-30), 0.0)
    o_c = _dot(vct_ref[0, 0], p_c.astype(BF16))
    for hp in range(HPG):
        part_ref[hp * HEAD_DIM:(hp + 1) * HEAD_DIM, :] = _gate_row(gate_ref, g, hp, 0) * o_c[:, hp * tq:(hp + 1) * tq]

    per_blk = SEL_BLOCK // CMP_STRIDE
    ns = nch // per_blk
    p_sum = p_c[:, 0:tq]
    for hp in range(1, HPG):
        p_sum = p_sum + p_c[:, hp * tq:(hp + 1) * tq]
    n_col = tq // LANES
    for c in range(n_col):
        psum_ref[c] = p_sum[:, c * LANES:(c + 1) * LANES]
    parts = [jnp.concatenate([psum_ref[c, pl.ds(o, ns, stride=per_blk), :] for c in range(n_col)], axis=1)
             for o in range(per_blk)]
    below = pltpu.roll(parts[per_blk - 1], 1, axis=0)
    below = jnp.where(lax.broadcasted_iota(jnp.int32, (ns, tq), 0) == 0, 0.0, below)
    imp = parts[0] + parts[1] + parts[2] + parts[3] + below
    nsp = selb_ref.shape[0]
    if nsp > ns:
        imp = jnp.concatenate([imp, jnp.zeros((nsp - ns, tq), F32)], axis=0)

    blk = lax.broadcasted_iota(jnp.int32, (nsp, tq), 0).astype(F32)
    r_lane = lax.broadcasted_iota(jnp.int32, (nsp, tq), 1)
    cur = ((tq // SEL_BLOCK) * qb + jnp.right_shift(r_lane, SEL_BLOCK.bit_length() - 1)).astype(F32)
    forced = (blk == 0.0) | (blk == cur) | (blk == cur - 1.0)
    valid = blk <= cur
    score = jnp.where(valid & jnp.logical_not(forced), imp, -jnp.inf)
    sel = forced
    for _ in range(SEL_TOPK - 3):
        mx = jnp.max(score, axis=0, keepdims=True)
        first = jnp.min(jnp.where(score == mx, blk, 1e9), axis=0, keepdims=True)
        hit = blk == first
        sel = sel | hit
        score = jnp.where(hit, -jnp.inf, score)
    selb_ref[...] = jnp.where(sel & valid, 0.0, NEG).astype(BF16)


def _nsa_select(zt, gt, kc, vct, u_tab, bsz, seq):
    tq = u_tab.shape[2]
    nq = seq // tq
    nch = seq // CMP_STRIDE
    gw = HPG * HEAD_DIM
    n = bsz * seq
    sel_rows = HEAD_DIM
    assert seq // SEL_BLOCK <= sel_rows
    return pl.pallas_call(
        _nsa_select_kernel,
        name="nsa_select",
        grid=(bsz, NSA_GROUPS, nq),
        in_specs=[pl.BlockSpec((gw, tq), lambda b, g, q: (ROW_QA // gw + g, b * nq + q)),
                  pl.BlockSpec((ROWS_SMALL, tq), lambda b, g, q: (0, b * nq + q)),
                  pl.BlockSpec((1, 1, nch, LANES), lambda b, g, q: (b, g, 0, 0)),
                  pl.BlockSpec((1, 1, HEAD_DIM, nch), lambda b, g, q: (b, g, 0, 0)),
                  pl.BlockSpec((HPG,) + u_tab.shape[1:], lambda b, g, q: (g, 0, 0))],
        out_specs=[pl.BlockSpec((sel_rows, tq), lambda b, g, q: (g, b * nq + q)),
                   pl.BlockSpec((gw, tq), lambda b, g, q: (g, b * nq + q))],
        out_shape=[jax.ShapeDtypeStruct((NSA_GROUPS * sel_rows, n), BF16),
                   jax.ShapeDtypeStruct((NSA_HEADS * HEAD_DIM, n), F32)],
        scratch_shapes=[pltpu.VMEM((nch, HPG * tq), F32), pltpu.VMEM((tq // LANES, nch, LANES), F32)],
        compiler_params=_params(("arbitrary", "arbitrary", "arbitrary")),
    )(zt, gt, kc, vct, u_tab)


def _nsa_kernel(q_ref, gate_ref, selb_ref, partc_ref, ks_ref, vst_ref, kw_ref, vwt_ref, oh_ref,
                bn_ref, band_ref, o_ref, sa_ref, sb_ref, *sw_refs):
    g = pl.program_id(1)
    qb = pl.program_id(2)
    seq = ks_ref.shape[0]
    tq = NSA_TQ
    wide = HPG * tq

    def lanes4(fn):
        return jnp.concatenate([fn(hp) for hp in range(HPG)], axis=1)

    qs = [q_ref[hp * HEAD_DIM:(hp + 1) * HEAD_DIM, :] for hp in range(HPG)]
    one_row = (lax.broadcasted_iota(jnp.int32, (HEAD_DIM, tq), 0) == 0).astype(BF16)
    q_pad = lanes4(lambda hp: jnp.concatenate([qs[hp], one_row], axis=0))
    key_bias_lane = (lax.broadcasted_iota(jnp.int32, (1, LANES), 1) == HEAD_DIM).astype(F32)
    bn_prev = lanes4(lambda hp: bn_ref[hp, 0:tq, :])
    bn_diag = lanes4(lambda hp: bn_ref[hp, tq:2 * tq, :])

    n_win = NSA_WINDOW // tq
    band4 = lanes4(lambda hp: band_ref[...])
    win_start = []
    for a in range(n_win + 1):
        ti = qb - n_win + a
        t0 = pl.multiple_of(jnp.maximum(ti, 0) * tq, tq)
        win_start.append(t0)
        k_a = kw_ref[pl.ds(t0, tq), :]
        if a < n_win:
            k_a = k_a + (key_bias_lane * jnp.where(ti >= 0, 0.0, NEG)).astype(BF16)
        s_a = _dot(k_a, q_pad)
        if a == 0:
            s_a = s_a + band4
        if a == n_win - 1:
            s_a = s_a + bn_prev
        if a == n_win:
            s_a = s_a + bn_diag
        sw_refs[a][...] = s_a

    selb = selb_ref[...]
    blk = lax.broadcasted_iota(jnp.int32, selb.shape, 0)
    selb_far = jnp.where(blk < (qb - 1) * (tq // SEL_BLOCK), selb, jnp.asarray(NEG, BF16))
    q_near = lanes4(lambda hp: jnp.concatenate([qs[hp], selb], axis=0))
    q_far = lanes4(lambda hp: jnp.concatenate([qs[hp], selb_far], axis=0))

    p0 = pl.multiple_of(jnp.maximum(qb - 1, 0) * tq, tq)
    d0 = pl.multiple_of(qb * tq, tq)
    p0_ids = pl.multiple_of(jnp.where(qb > 0, p0, seq - tq), tq)
    kp = ks_ref[pl.ds(p0, tq), :] + oh_ref[pl.ds(p0_ids, tq), :]
    kd = ks_ref[pl.ds(d0, tq), :] + oh_ref[pl.ds(d0, tq), :]
    sn_prev, sn_diag = sw_refs[n_win + 1], sw_refs[n_win + 2]
    sn_prev[...] = _dot(kp, q_near) + bn_prev
    sn_diag[...] = _dot(kd, q_near) + bn_diag

    carry_w = _softmax_init(wide)
    for a in range(n_win + 1):
        carry_w = _softmax_update(sw_refs[a][...], *carry_w, vwt_ref[:, pl.ds(win_start[a], tq)])
    o_w = _softmax_finish(carry_w[1])

    carry = _softmax_update(sn_prev[...], *_softmax_init(wide), vst_ref[:, pl.ds(p0, tq)])
    carry = _softmax_update(sn_diag[...], *carry, vst_ref[:, pl.ds(d0, tq)])

    def far_scores(tile, s_ref):
        k0 = pl.multiple_of(jnp.minimum(tile * SEL_TK, seq - SEL_TK), SEL_TK)
        s_ref[...] = _dot(ks_ref[pl.ds(k0, SEL_TK), :] + oh_ref[pl.ds(k0, SEL_TK), :], q_far)

    def far_consume(s_ref, tile, carry):
        k0 = pl.multiple_of(tile * SEL_TK, SEL_TK)
        return _softmax_update(s_ref[...], *carry, vst_ref[:, pl.ds(k0, SEL_TK)])

    def far_body(j, carry):
        far_scores(2 * j + 1, sb_ref)
        carry = far_consume(sa_ref, 2 * j, carry)
        far_scores(2 * j + 2, sa_ref)
        return far_consume(sb_ref, 2 * j + 1, carry)

    far_tiles = 2 * SEL_TK // tq
    n_far = (jnp.maximum(qb - 1, 0) + far_tiles - 1) // far_tiles
    far_scores(0, sa_ref)
    _, acc_s = lax.fori_loop(0, n_far, far_body, carry)
    o_s = _softmax_finish(acc_s)

    for hp in range(HPG):
        rows = slice(hp * HEAD_DIM, (hp + 1) * HEAD_DIM)
        sl = slice(hp * tq, (hp + 1) * tq)
        out = partc_ref[rows, :] + _gate_row(gate_ref, g, hp, 1) * o_s[:, sl] + _gate_row(gate_ref, g, hp, 2) * o_w[:, sl]
        o_ref[rows, :] = out.astype(BF16)


def _nsa(zt, zk, gt, selb, partc, onehot, bn_tab, band0, bsz, seq):
    nq = seq // NSA_TQ
    gw = HPG * HEAD_DIM
    n = bsz * seq
    nsp = selb.shape[0] // NSA_GROUPS
    const2 = lambda shp: pl.BlockSpec(shp, lambda b, g, q: (0, 0))
    return pl.pallas_call(
        _nsa_kernel,
        name="nsa_attn",
        grid=(bsz, NSA_GROUPS, nq),
        in_specs=[pl.BlockSpec((gw, NSA_TQ), lambda b, g, q: (ROW_QA // gw + g, b * nq + q)),
                  pl.BlockSpec((ROWS_SMALL, NSA_TQ), lambda b, g, q: (0, b * nq + q)),
                  pl.BlockSpec((nsp, NSA_TQ), lambda b, g, q: (g, b * nq + q)),
                  pl.BlockSpec((gw, NSA_TQ), lambda b, g, q: (g, b * nq + q)),
                  pl.BlockSpec((seq, LANES), lambda b, g, q: (b, CB_KSLC + g)),
                  pl.BlockSpec((HEAD_DIM, seq), lambda b, g, q: (ROW_VSLC // HEAD_DIM + g, b)),
                  pl.BlockSpec((seq, LANES), lambda b, g, q: (b, CB_KWIN + g)),
                  pl.BlockSpec((HEAD_DIM, seq), lambda b, g, q: (ROW_VWIN // HEAD_DIM + g, b)),
                  const2(onehot.shape),
                  pl.BlockSpec((HPG,) + bn_tab.shape[1:], lambda b, g, q: (g, 0, 0)),
                  const2(band0.shape)],
        out_specs=pl.BlockSpec((gw, NSA_TQ), lambda b, g, q: (g, b * nq + q)),
        out_shape=jax.ShapeDtypeStruct((NSA_HEADS * HEAD_DIM, n), BF16),
        scratch_shapes=[pltpu.VMEM((SEL_TK, HPG * NSA_TQ), F32), pltpu.VMEM((SEL_TK, HPG * NSA_TQ), F32)]
        + [pltpu.VMEM((NSA_TQ, HPG * NSA_TQ), F32) for _ in range(NSA_WINDOW // NSA_TQ + 3)],
        compiler_params=_params(("arbitrary", "arbitrary", "arbitrary")),
    )(zt, gt, selb, partc, zk, zt, zk, zt, onehot, bn_tab, band0)


SWA_SUB = 4


def _swa_kernel(layer, sink_ref, q_ref, kcur_ref, kprev_ref, vcur_ref, vprev_ref, bs_ref, o_ref, *s_refs):
    g = pl.program_id(1)
    t = pl.program_id(2)
    tq = SWA_WINDOW
    zeros_q = jnp.zeros((HEAD_DIM, tq), BF16)

    def lanes4(fn):
        return jnp.concatenate([fn(hp) for hp in range(HPG)], axis=1)

    bs_prev = lanes4(lambda hp: bs_ref[hp, 0:tq, :])
    bs_diag = lanes4(lambda hp: bs_ref[hp, tq:2 * tq, :])
    sink = lanes4(lambda hp: jnp.full((1, tq), sink_ref[layer, g * HPG + hp] * LOG2E, F32))
    for sub in range(SWA_SUB):
        cs = slice(sub * tq, (sub + 1) * tq)
        q_pad = lanes4(lambda hp: jnp.concatenate(
            [q_ref[hp * HEAD_DIM:(hp + 1) * HEAD_DIM, cs], zeros_q], axis=0))
        kp = kprev_ref[...] if sub == 0 else kcur_ref[(sub - 1) * tq:sub * tq, :]
        s_p = _dot(kp, q_pad) + bs_prev
        if sub == 0:
            s_p = jnp.where(t > 0, s_p, NEG)
        s_refs[2 * sub][...] = s_p
        s_refs[2 * sub + 1][...] = _dot(kcur_ref[cs, :], q_pad) + bs_diag
    init = (sink, jnp.concatenate([jnp.zeros((HEAD_DIM, HPG * tq), F32), jnp.ones((SUM_ROWS, HPG * tq), F32)], axis=0))
    for sub in range(SWA_SUB):
        cs = slice(sub * tq, (sub + 1) * tq)
        vp = vprev_ref[...] if sub == 0 else vcur_ref[:, (sub - 1) * tq:sub * tq]
        carry = _softmax_update(s_refs[2 * sub][...], *init, vp)
        _, acc = _softmax_update(s_refs[2 * sub + 1][...], *carry, vcur_ref[:, cs])
        o = _softmax_finish(acc)
        for hp in range(HPG):
            o_ref[hp * HEAD_DIM:(hp + 1) * HEAD_DIM, cs] = o[:, hp * tq:(hp + 1) * tq].astype(BF16)


def _swa(zt, zk, layer, sinks, bs_tab, bsz, seq):
    tq = SWA_WINDOW
    nq = seq // tq
    nt = nq // SWA_SUB
    wide = SWA_SUB * tq
    gw = HPG * HEAD_DIM
    n = bsz * seq
    prev = lambda b, t: b * nq + jnp.maximum(t * SWA_SUB - 1, 0)
    return pl.pallas_call(
        functools.partial(_swa_kernel, layer),
        name="swa_attn",
        grid=(bsz, SWA_GROUPS, nt),
        in_specs=[pl.BlockSpec(memory_space=pltpu.SMEM),
                  pl.BlockSpec((gw, wide), lambda b, g, t: (ROW_QB // gw + g, b * nt + t)),
                  pl.BlockSpec((wide, LANES), lambda b, g, t: (b * nt + t, CB_KB + g)),
                  pl.BlockSpec((tq, LANES), lambda b, g, t: (prev(b, t), CB_KB + g)),
                  pl.BlockSpec((HEAD_DIM, wide), lambda b, g, t: (ROW_VB // HEAD_DIM + g, b * nt + t)),
                  pl.BlockSpec((HEAD_DIM, tq), lambda b, g, t: (ROW_VB // HEAD_DIM + g, prev(b, t))),
                  pl.BlockSpec((HPG,) + bs_tab.shape[1:], lambda b, g, t: (g, 0, 0))],
        out_specs=pl.BlockSpec((gw, wide), lambda b, g, t: (g, b * nt + t)),
        out_shape=jax.ShapeDtypeStruct((SWA_HEADS * HEAD_DIM, n), BF16),
        scratch_shapes=[pltpu.VMEM((tq, HPG * tq), F32) for _ in range(2 * SWA_SUB)],
        compiler_params=_params(("arbitrary", "arbitrary", "arbitrary")),
    )(sinks, zt, zk, zk, zt, zt, bs_tab)


def _fox_keys_kernel(fz_ref, bf_ref, k_ref, o_ref, c_ref):
    seq = fz_ref.shape[0]
    ch = LANES
    logf = jax.nn.log_sigmoid(fz_ref[...] + bf_ref[...])
    tri = (lax.broadcasted_iota(jnp.int32, (ch, ch), 0) >= lax.broadcasted_iota(jnp.int32, (ch, ch), 1)).astype(F32)
    carry = jnp.zeros((1, LANES), F32)
    for i in range(seq // ch):
        c = jnp.dot(tri, logf[i * ch:(i + 1) * ch, :], preferred_element_type=F32,
                    precision=lax.Precision.HIGHEST) + carry
        c_ref[i * ch:(i + 1) * ch, :] = c
        carry = c[ch - 1:ch, :]
    neg = c_ref[...] * (-LOG2E)
    hi = neg.astype(BF16)
    r1 = neg - hi.astype(F32)
    mid = r1.astype(BF16)
    lo = (r1 - mid.astype(F32)).astype(BF16)
    parts = jnp.concatenate([hi, mid, lo], axis=1)
    wide = FOX_HEADS * LANES
    row = lax.broadcasted_iota(jnp.int32, (3 * LANES, wide), 0)
    col = lax.broadcasted_iota(jnp.int32, (3 * LANES, wide), 1)
    head, lane = jnp.right_shift(col, LANES.bit_length() - 1), jnp.bitwise_and(col, LANES - 1)
    term, src = jnp.right_shift(row, LANES.bit_length() - 1), jnp.bitwise_and(row, LANES - 1)
    place = (src == COL_FC + head) & (lane == LANE_C + term)
    bias = _dot(parts, place.astype(BF16)).astype(BF16)
    key_lane = jnp.bitwise_and(lax.broadcasted_iota(jnp.int32, (seq, wide), 1), LANES - 1)
    o_ref[...] = jnp.where(key_lane < HEAD_DIM, k_ref[...], bias)


def _fox_keys(fz, layer, bf_rows, zk, bsz, seq):
    wide = FOX_HEADS * LANES
    return pl.pallas_call(
        _fox_keys_kernel,
        name="fox_keys",
        grid=(bsz,),
        in_specs=[pl.BlockSpec((seq, LANES), lambda b: (b, 0)),
                  _layer_spec(bf_rows, layer),
                  pl.BlockSpec((seq, wide), lambda b: (b, CB_KC))],
        out_specs=pl.BlockSpec((seq, wide), lambda b: (b, 0)),
        out_shape=jax.ShapeDtypeStruct((bsz * seq, wide), BF16),
        scratch_shapes=[pltpu.VMEM((seq, LANES), F32)],
        compiler_params=_params(("arbitrary",)),
    )(fz, bf_rows, zk)


FOX_T = 512
FOX_HPS = 2


def _fox_kernel(q_ref, k_ref, vt_ref, o_ref, *s_refs):
    qi = pl.program_id(2)
    t = q_ref.shape[1]
    tk = t // 2
    ones_rows = (lax.broadcasted_iota(jnp.int32, (HEAD_DIM, t), 0) < 3).astype(BF16)
    rows = lambda hh: slice(hh * HEAD_DIM, (hh + 1) * HEAD_DIM)
    q_aug = [jnp.concatenate([q_ref[rows(hh), :], ones_rows], axis=0) for hh in range(FOX_HPS)]

    def scores(tile, slot):
        k0 = pl.multiple_of(tile * tk, tk)
        for hh in range(FOX_HPS):
            s_refs[2 * hh + slot][...] = _dot(k_ref[pl.ds(k0, tk), hh * LANES:(hh + 1) * LANES], q_aug[hh])

    def consume(slot, tile, carry, first_masked_query=None):
        k0 = pl.multiple_of(tile * tk, tk)
        out = []
        for hh in range(FOX_HPS):
            s = s_refs[2 * hh + slot][...]
            if first_masked_query is not None:
                key = lax.broadcasted_iota(jnp.int32, (tk, t), 0) + first_masked_query
                s = jnp.where(key > lax.broadcasted_iota(jnp.int32, (tk, t), 1), NEG, s)
            out.append(_softmax_update(s, *carry[hh], vt_ref[rows(hh), pl.ds(k0, tk)]))
        return tuple(out)

    def body(j, carry):
        scores(2 * j + 1, 1)
        carry = consume(0, 2 * j, carry)
        scores(2 * j + 2, 0)
        return consume(1, 2 * j + 1, carry)

    scores(0, 0)
    carry = lax.fori_loop(0, qi, body, tuple(_softmax_init(t) for _ in range(FOX_HPS)))
    scores(2 * qi + 1, 1)
    carry = consume(0, 2 * qi, carry, 0)
    carry = consume(1, 2 * qi + 1, carry, tk)
    for hh in range(FOX_HPS):
        o_ref[rows(hh), :] = _softmax_finish(carry[hh][1]).astype(BF16)


def _fox(zt, kaug, bsz, seq):
    t = min(FOX_T, seq)
    nq = seq // t
    n = bsz * seq
    hw = FOX_HPS * HEAD_DIM
    return pl.pallas_call(
        _fox_kernel,
        name="fox_attn",
        grid=(bsz, FOX_HEADS // FOX_HPS, nq),
        in_specs=[pl.BlockSpec((hw, t), lambda b, h, q: (ROW_QC // hw + h, b * nq + q)),
                  pl.BlockSpec((seq, FOX_HPS * LANES), lambda b, h, q: (b, h)),
                  pl.BlockSpec((hw, seq), lambda b, h, q: (ROW_VC // hw + h, b))],
        out_specs=pl.BlockSpec((hw, t), lambda b, h, q: (h, b * nq + q)),
        out_shape=jax.ShapeDtypeStruct((FOX_HEADS * HEAD_DIM, n), BF16),
        scratch_shapes=[pltpu.VMEM((t // 2, t), F32) for _ in range(2 * FOX_HPS)],
        compiler_params=_params(("arbitrary", "arbitrary", "arbitrary")),
    )(zt, kaug, zt)


MERGE_TM = 512


def _merge_ln_kernel(x_ref, xb_ref, oa_ref, ob_ref, oc_ref, wg_ref, bg_ref, wbr_ref, wo_ref, g_ref, b_ref,
                     o_ref, obf_ref):
    d = x_ref.shape[1]
    xb = xb_ref[...]
    merged = None
    for i, br_ref in enumerate((oa_ref, ob_ref, oc_ref)):
        gate = jax.nn.sigmoid(_dot(xb, wg_ref[:, i * d:(i + 1) * d]) + bg_ref[:, i * d:(i + 1) * d])
        term = gate * _dot_tn(br_ref[...], wbr_ref[i])
        merged = term if merged is None else merged + term
    mix = _dot(merged.astype(BF16), wo_ref[...])
    out = _layer_norm(DN_ALPHA * x_ref[...] + mix, g_ref[...], b_ref[...])
    o_ref[...] = out
    obf_ref[...] = out.astype(BF16)


def _merge_ln(x, xb, ota, otb, otc, layer, wg, bg, wbr, wo, g, b):
    n, d = x.shape
    tm = min(MERGE_TM, n)
    w = ota.shape[0]
    tok = pl.BlockSpec((tm, d), lambda i: (i, 0))
    feat = pl.BlockSpec((w, tm), lambda i: (0, i))
    return pl.pallas_call(
        _merge_ln_kernel,
        name="merge_ln",
        grid=(n // tm,),
        in_specs=[tok, tok, feat, feat, feat]
        + [_layer_spec(a, layer, single_buffer=True) for a in (wg, bg, wbr, wo, g, b)],
        out_specs=[tok, tok],
        out_shape=[jax.ShapeDtypeStruct((n, d), F32), jax.ShapeDtypeStruct((n, d), BF16)],
        compiler_params=_params(("arbitrary",)),
    )(x, xb, ota, otb, otc, wg, bg, wbr, wo, g, b)


def _split_w_in(w_in):
    sizes = [NSA_HEADS * HEAD_DIM, 6 * NSA_GROUPS * HEAD_DIM, N_GA, SWA_HEADS * HEAD_DIM,
             SWA_GROUPS * HEAD_DIM, SWA_GROUPS * HEAD_DIM, FOX_HEADS * HEAD_DIM, FOX_HEADS * HEAD_DIM,
             FOX_HEADS * HEAD_DIM, FOX_HEADS]
    q_a, kv_a, g_a, q_b, k_b, v_b, q_c, k_c, v_c, f_c = jnp.split(w_in, np.cumsum(sizes)[:-1].tolist(), axis=-1)
    k_cmp, v_cmp, k_slc, v_slc, k_win, v_win = jnp.split(kv_a, 6, axis=-1)
    depth, d = w_in.shape[0], w_in.shape[1]

    def pad_heads(w):
        nh = w.shape[-1] // HEAD_DIM
        w = w.reshape(depth, d, nh, HEAD_DIM)
        return jnp.concatenate([w, jnp.zeros_like(w)], axis=-1).reshape(depth, d, nh * LANES)

    w_feat_t = jnp.swapaxes(jnp.concatenate([q_a, q_b, q_c, v_slc, v_win, v_b, v_c], axis=-1), 1, 2)
    w_k = jnp.concatenate([pad_heads(k_c), pad_heads(k_slc), pad_heads(k_win), pad_heads(k_b)], axis=-1)
    w_cmp = jnp.concatenate([k_cmp, v_cmp], axis=-1)
    small = jnp.concatenate([g_a, f_c], axis=-1)
    w_small = jnp.concatenate([small, jnp.zeros((depth, d, LANES - ROWS_SMALL), small.dtype)], axis=-1)
    w_small_t = jnp.swapaxes(small, 1, 2)
    return w_feat_t, w_k, w_cmp, w_small, w_small_t


def kernel(x, rel_bias, ln1_g, ln1_b, ffn1_w1, ffn1_w2, w_in, cmp_pe_k, cmp_pe_v, cmp_k_w1, cmp_k_w2, cmp_v_w1,
           cmp_v_w2, swa_sinks, fox_b_f, w_br_a, w_br_b, w_br_c, w_gate, b_gate, w_out, ln2_g, ln2_b, ffn2_w1,
           ffn2_w2, ln3_g, ln3_b):
    bsz, seq, d = x.shape
    n = bsz * seq
    depth = w_in.shape[0]
    nch = seq // CMP_STRIDE
    tabs = _static_tables(seq)
    onehot = jnp.asarray(tabs["onehot"], BF16)
    band0 = jnp.asarray(tabs["band0"])
    u_tab, bn_tab, bs_tab = _bias_tables(rel_bias, tabs)

    bf = lambda w: w.astype(BF16)
    w_feat_t, w_k, w_cmp, w_small, w_small_t = [bf(w) for w in _split_w_in(w_in)]
    f1w1, f1w2, f2w1, f2w2 = bf(ffn1_w1), bf(ffn1_w2), bf(ffn2_w1), bf(ffn2_w2)
    ck1, ck2, cv1 = bf(cmp_k_w1), bf(cmp_k_w2), bf(cmp_v_w1)
    cv2t = bf(jnp.swapaxes(cmp_v_w2, 1, 2))
    pek = cmp_pe_k.reshape(depth, 1, CMP_BLOCK * HEAD_DIM)
    pev = cmp_pe_v.reshape(depth, 1, CMP_BLOCK * HEAD_DIM)
    wg, wo = bf(w_gate), bf(w_out)
    wbr = bf(jnp.stack([w_br_a, w_br_b, w_br_c], axis=1))
    bf_rows = jnp.zeros((depth, 1, LANES), F32).at[:, 0, COL_FC:COL_FC + FOX_HEADS].set(fox_b_f)
    rows = lambda v: v.reshape(depth, 1, -1)
    ln1, ln2, ln3 = (rows(ln1_g), rows(ln1_b)), (rows(ln2_g), rows(ln2_b)), (rows(ln3_g), rows(ln3_b))
    bg = rows(b_gate)

    h = x.reshape(n, d)
    for l in range(depth):
        h, hb = _ffn_ln(h, l, f1w1, f1w2, *ln1)
        zt, zk, zc, fz, gt = _project(hb, l, w_feat_t, w_k, w_cmp, w_small, w_small_t)
        zc4 = zc.reshape(bsz, nch, CMP_STRIDE, 2 * NSA_GROUPS, HEAD_DIM).transpose(0, 3, 1, 2, 4)
        zc4 = zc4.reshape(bsz, 2 * NSA_GROUPS, nch, CMP_STRIDE * HEAD_DIM)
        kc, vct = _compress(zc4, l, pek, pev, ck1, ck2, cv1, cv2t)
        selb, part_c = _nsa_select(zt, gt, kc, vct, u_tab, bsz, seq)
        ot_a = _nsa(zt, zk, gt, selb, part_c, onehot, bn_tab, band0, bsz, seq)
        ot_b = _swa(zt, zk, l, swa_sinks, bs_tab, bsz, seq)
        kaug = _fox_keys(fz, l, bf_rows, zk, bsz, seq)
        ot_c = _fox(zt, kaug, bsz, seq)
        h, hb = _merge_ln(h, hb, ot_a, ot_b, ot_c, l, wg, bg, wbr, wo, *ln2)
        h, hb = _ffn_ln(h, l, f2w1, f2w2, *ln3)
    return h.reshape(bsz, seq, d)
```

```python
import functools
import math

import numpy as np
import jax
import jax.numpy as jnp
from jax import lax
from jax.experimental import pallas as pl
from jax.experimental.pallas import tpu as pltpu

F32 = jnp.float32
BF16 = jnp.bfloat16

D_MODEL = 1024
DEPTH = 4
HEAD_DIM = 64
NSA_HEADS = 8
NSA_GROUPS = 2
HPG = NSA_HEADS // NSA_GROUPS
CMP_BLOCK = 32
CMP_STRIDE = 16
CMP_HIDDEN = 256
SEL_BLOCK = 64
SEL_TOPK = 8
NSA_WINDOW = 512
SWA_HEADS = 8
SWA_GROUPS = 2
SWA_WINDOW = 128
FOX_HEADS = 8
NSA_TQ = 256
NSA_SEL_TQ = 512
REL_BUCKETS = 32
REL_MAX_DIST = 128
D_FF = 2816
LN_EPS = 1e-5
DN_ALPHA = (2 * DEPTH) ** 0.25
SCALE = HEAD_DIM ** -0.5
LOG2E = math.log2(math.e)
Q_SCALE = SCALE * LOG2E
SUM_ROWS = 16

NEG = -1e30
NEG_TEST = -1e29
LANES = 128
VMEM_LIMIT = 56 * 1024 * 1024

ROW_QA, ROW_QB, ROW_QC = 0, 512, 1024
ROW_VC, ROW_VSLC, ROW_VWIN, ROW_VB = 1536, 2048, 2176, 2304
ROWS_Q = ROW_VC
ROWS_FEAT = 2432
CB_KC, CB_KSLC, CB_KWIN, CB_KB = 0, 8, 10, 12
COLS_K = 14 * LANES
N_GA = 3 * NSA_HEADS
COL_FC = N_GA
ROWS_SMALL = 32
LANE_C = HEAD_DIM


def _params(sem, vmem=VMEM_LIMIT):
    return pltpu.CompilerParams(dimension_semantics=sem, vmem_limit_bytes=vmem)


def _layer_spec(w, layer, single_buffer=False):
    zeros = (0,) * (w.ndim - 1)
    mode = dict(pipeline_mode=pl.Buffered(1)) if single_buffer else {}
    return pl.BlockSpec((None,) + w.shape[1:], lambda *_: (layer,) + zeros, **mode)


def _dot(a, b):
    return jnp.dot(a, b, preferred_element_type=F32)


def _dot_nt(a, b):
    return lax.dot_general(a, b, (((1,), (1,)), ((), ())), preferred_element_type=F32)


def _dot_tn(a, b):
    return lax.dot_general(a, b, (((0,), (0,)), ((), ())), preferred_element_type=F32)


def _layer_norm(y, g, b):
    mu = jnp.mean(y, axis=-1, keepdims=True)
    yc = y - mu
    var = jnp.mean(yc * yc, axis=-1, keepdims=True)
    return yc * lax.rsqrt(var + LN_EPS) * g + b


def _bucket_np(n):
    n = np.maximum(n, 0)
    exact = REL_BUCKETS // 2
    ratio = np.log(np.maximum(n, 1).astype(np.float64) / exact) / math.log(REL_MAX_DIST / exact)
    large = exact + (ratio * (REL_BUCKETS - exact)).astype(np.int64)
    return np.where(n < exact, n, np.minimum(large, REL_BUCKETS - 1)).astype(np.int32)


def _static_tables(seq):
    nch = seq // CMP_STRIDE
    ns = seq // SEL_BLOCK

    def near_buckets(tq):
        r = np.arange(tq)[None, :]
        c = np.arange(tq)[:, None]
        d_prev = tq + r - c
        d_diag = r - c
        return d_prev, _bucket_np(d_prev), np.where(d_diag >= 0, _bucket_np(d_diag), -1)

    r = np.arange(NSA_TQ)[None, :]
    r_sel = np.arange(min(NSA_SEL_TQ, seq))[None, :]
    u = np.arange(2 * nch)[:, None]
    n_c = r_sel - (CMP_BLOCK - 1) - CMP_STRIDE * (u - nch)
    bk_c = np.where(n_c >= 0, _bucket_np(n_c), -1).astype(np.int32)
    _, bk_prev, bk_diag = near_buckets(NSA_TQ)
    bk_near = np.concatenate([bk_prev, bk_diag], axis=0).astype(np.int32)
    d_prev, bk_prev, bk_diag = near_buckets(SWA_WINDOW)
    bk_swa = np.concatenate([np.where(d_prev < SWA_WINDOW, bk_prev, -1), bk_diag], axis=0).astype(np.int32)
    band0 = np.where(np.arange(NSA_TQ)[:, None] > r, 0.0, NEG).astype(np.float32)
    s = np.arange(seq)[:, None]
    lane = np.arange(LANES)[None, :]
    onehot = (lane == HEAD_DIM + s // SEL_BLOCK).astype(np.float32)
    return dict(bk_c=bk_c, bk_near=bk_near, bk_swa=bk_swa, band0=band0, onehot=onehot)


def _tables_kernel(rel_ref, bkc_ref, bkn_ref, bks_ref, u_ref, bn_ref, bs_ref):
    h = pl.program_id(0)

    def build(bk, col, delta):
        base = rel_ref[REL_BUCKETS - 1, col] if delta else 0.0
        acc = jnp.zeros(bk.shape, F32)
        for b in range(REL_BUCKETS):
            acc = jnp.where(bk == b, (rel_ref[b, col] - base) * LOG2E, acc)
        return jnp.where(bk < 0, NEG, acc)

    u_ref[0] = build(bkc_ref[...], h, True)
    bn_ref[0] = build(bkn_ref[...], h, True)
    bs_ref[0] = build(bks_ref[...], NSA_HEADS + h, False)


def _bias_tables(rel_bias, tabs):
    buckets = [jnp.asarray(tabs[k]) for k in ("bk_c", "bk_near", "bk_swa")]
    full = lambda a: pl.BlockSpec(a.shape, lambda h: (0, 0))
    per_head = lambda a: pl.BlockSpec((1,) + a.shape, lambda h: (h, 0, 0))
    return pl.pallas_call(
        _tables_kernel,
        name="bias_tables",
        grid=(NSA_HEADS,),
        in_specs=[pl.BlockSpec(memory_space=pltpu.SMEM)] + [full(a) for a in buckets],
        out_specs=[per_head(a) for a in buckets],
        out_shape=[jax.ShapeDtypeStruct((NSA_HEADS,) + a.shape, F32) for a in buckets],
        compiler_params=_params(("arbitrary",)),
    )(rel_bias, *buckets)


FFN_TM = 1024
FFN_TF = 256


def _ffn_ln_kernel(x_ref, w1_ref, w2_ref, g_ref, b_ref, o_ref, ob_ref, act_ref):
    f = w2_ref.shape[0]
    x = x_ref[...]
    xb = x.astype(BF16)
    for c in range(f // FFN_TF):
        gt = _dot(xb, w1_ref[:, c * FFN_TF:(c + 1) * FFN_TF])
        up = _dot(xb, w1_ref[:, f + c * FFN_TF:f + (c + 1) * FFN_TF])
        act_ref[:, c * FFN_TF:(c + 1) * FFN_TF] = (gt * jax.nn.sigmoid(gt) * up).astype(BF16)
    y = DN_ALPHA * x + 0.5 * _dot(act_ref[...], w2_ref[...])
    out = _layer_norm(y, g_ref[...], b_ref[...])
    o_ref[...] = out
    ob_ref[...] = out.astype(BF16)


def _ffn_ln(x, layer, w1, w2, g, b):
    n, d = x.shape
    f = w2.shape[1]
    tm = min(FFN_TM, n)
    resident = lambda a: _layer_spec(a, layer, single_buffer=True)
    return pl.pallas_call(
        _ffn_ln_kernel,
        name="ffn_ln",
        grid=(n // tm,),
        in_specs=[pl.BlockSpec((tm, d), lambda i: (i, 0)), resident(w1), resident(w2), resident(g), resident(b)],
        out_specs=[pl.BlockSpec((tm, d), lambda i: (i, 0)), pl.BlockSpec((tm, d), lambda i: (i, 0))],
        out_shape=[jax.ShapeDtypeStruct((n, d), F32), jax.ShapeDtypeStruct((n, d), BF16)],
        scratch_shapes=[pltpu.VMEM((tm, f), BF16)],
        compiler_params=_params(("arbitrary",)),
    )(x, w1, w2, g, b)


PROJ_TM = 1024


def _proj_kernel(xb_ref, wft_ref, wk_ref, wc_ref, ws_ref, wst_ref, zt_ref, zk_ref, zc_ref, fz_ref, gt_ref):
    xb = xb_ref[...]
    zt_ref[:ROWS_Q, :] = (_dot_nt(wft_ref[:ROWS_Q, :], xb) * Q_SCALE).astype(BF16)
    zt_ref[ROWS_Q:, :] = _dot_nt(wft_ref[ROWS_Q:, :], xb).astype(BF16)
    zk_ref[...] = _dot(xb, wk_ref[...]).astype(BF16)
    zc_ref[...] = _dot(xb, wc_ref[...]).astype(BF16)
    fz_ref[...] = _dot(xb, ws_ref[...])
    gt_ref[...] = _dot_nt(wst_ref[...], xb)


def _project(xb, layer, wft, wk, wc, ws, wst):
    n, d = xb.shape
    tm = min(PROJ_TM, n)
    return pl.pallas_call(
        _proj_kernel,
        name="proj_in",
        grid=(n // tm,),
        in_specs=[pl.BlockSpec((tm, d), lambda i: (i, 0))] + [_layer_spec(w, layer) for w in (wft, wk, wc, ws, wst)],
        out_specs=[pl.BlockSpec((ROWS_FEAT, tm), lambda i: (0, i)),
                   pl.BlockSpec((tm, COLS_K), lambda i: (i, 0)),
                   pl.BlockSpec((tm, 4 * HEAD_DIM), lambda i: (i, 0)),
                   pl.BlockSpec((tm, LANES), lambda i: (i, 0)),
                   pl.BlockSpec((ROWS_SMALL, tm), lambda i: (0, i))],
        out_shape=[jax.ShapeDtypeStruct((ROWS_FEAT, n), BF16),
                   jax.ShapeDtypeStruct((n, COLS_K), BF16),
                   jax.ShapeDtypeStruct((n, 4 * HEAD_DIM), BF16),
                   jax.ShapeDtypeStruct((n, LANES), F32),
                   jax.ShapeDtypeStruct((ROWS_SMALL, n), F32)],
        compiler_params=_params(("arbitrary",)),
    )(xb, wft, wk, wc, ws, wst)


def _gelu_tanh(x):
    return 0.5 * x * (1.0 + jnp.tanh(math.sqrt(2.0 / math.pi) * (x + 0.044715 * (x * x * x))))


def _compress_kernel(ck_ref, cv_ref, pek_ref, pev_ref, wk1_ref, wk2_ref, wv1_ref, wv2t_ref, kc_ref, vct_ref):
    half = CMP_STRIDE * HEAD_DIM
    nch = ck_ref.shape[2]

    def hidden(c_ref, pe_ref, w1_ref):
        c = c_ref[0, 0].astype(F32)
        top = (c + pe_ref[:, :half]).astype(BF16)
        bot = (c + pe_ref[:, half:]).astype(BF16)
        a = _dot(top, w1_ref[:half, :])
        bm = _dot(bot, w1_ref[half:, :])
        h = a + pltpu.roll(bm, nch - 1, axis=0)
        return _gelu_tanh(h).astype(BF16)

    kc = _dot(hidden(ck_ref, pek_ref, wk1_ref), wk2_ref[...])
    kc_ref[0, 0] = jnp.concatenate([kc, jnp.zeros_like(kc)], axis=1).astype(BF16)
    vct_ref[0, 0] = _dot_nt(wv2t_ref[...], hidden(cv_ref, pev_ref, wv1_ref)).astype(BF16)


def _compress(zc4, layer, pek, pev, wk1, wk2, wv1, wv2t):
    bsz, _, nch, wide = zc4.shape
    return pl.pallas_call(
        _compress_kernel,
        name="nsa_compress",
        grid=(bsz, NSA_GROUPS),
        in_specs=[pl.BlockSpec((1, 1, nch, wide), lambda b, g: (b, g, 0, 0)),
                  pl.BlockSpec((1, 1, nch, wide), lambda b, g: (b, NSA_GROUPS + g, 0, 0))]
        + [_layer_spec(w, layer) for w in (pek, pev, wk1, wk2, wv1, wv2t)],
        out_specs=[pl.BlockSpec((1, 1, nch, LANES), lambda b, g: (b, g, 0, 0)),
                   pl.BlockSpec((1, 1, HEAD_DIM, nch), lambda b, g: (b, g, 0, 0))],
        out_shape=[jax.ShapeDtypeStruct((bsz, NSA_GROUPS, nch, LANES), BF16),
                   jax.ShapeDtypeStruct((bsz, NSA_GROUPS, HEAD_DIM, nch), BF16)],
        compiler_params=_params(("arbitrary", "arbitrary")),
    )(zc4, zc4, pek, pev, wk1, wk2, wv1, wv2t)


SEL_TK = 256


def _with_sum_rows(vt):
    return jnp.concatenate([vt, jnp.ones((SUM_ROWS, vt.shape[1]), vt.dtype)], axis=0)


def _softmax_update(s, m, acc, vt):
    mn = jnp.maximum(m, jnp.max(s, axis=0, keepdims=True))
    p = jnp.exp2(s - mn).astype(BF16)
    acc = jnp.exp2(m - mn) * acc + _dot(_with_sum_rows(vt), p)
    return mn, acc


def _softmax_init(width):
    return jnp.full((1, width), -jnp.inf, F32), jnp.zeros((HEAD_DIM + SUM_ROWS, width), F32)


def _softmax_finish(acc):
    return acc[:HEAD_DIM] * (1.0 / acc[HEAD_DIM:HEAD_DIM + 1])


def _gate_row(gate_ref, g, hp, branch):
    return jax.nn.sigmoid(gate_ref[pl.ds(g * (HPG * 3) + hp * 3 + branch, 1), :])


def _nsa_select_kernel(q_ref, gate_ref, kc_ref, vct_ref, u_ref, selb_ref, part_ref, sc_ref, psum_ref):
    g = pl.program_id(1)
    qb = pl.program_id(2)
    nch = kc_ref.shape[2]
    tq = q_ref.shape[1]

    def lanes4(fn):
        return jnp.concatenate([fn(hp) for hp in range(HPG)], axis=1)

    zeros_q = jnp.zeros((HEAD_DIM, tq), BF16)
    q_pad = lanes4(lambda hp: jnp.concatenate([q_ref[hp * HEAD_DIM:(hp + 1) * HEAD_DIM, :], zeros_q], axis=0))

    off = pl.multiple_of(nch - (tq // CMP_STRIDE) * qb, 8)
    sc_ref[...] = _dot(kc_ref[0, 0], q_pad) + lanes4(lambda hp: u_ref[hp, pl.ds(off, nch), :])
    s_c = sc_ref[...]
    m_c = jnp.max(s_c, axis=0, keepdims=True)
    p_c = jnp.exp2(s_c - m_c)
    l_c = jnp.sum(p_c, axis=0, keepdims=True)
    p_c = p_c * jnp.where(m_c > NEG_TEST, 1.0 / jnp.maximum(l_c, 1e-30), 0.0)
    o_c = _dot(vct_ref[0, 0], p_c.astype(BF16))
    for hp in range(HPG):
        part_ref[hp * HEAD_DIM:(hp + 1) * HEAD_DIM, :] = _gate_row(gate_ref, g, hp, 0) * o_c[:, hp * tq:(hp + 1) * tq]

    per_blk = SEL_BLOCK // CMP_STRIDE
    ns = nch // per_blk
    p_sum = p_c[:, 0:tq]
    for hp in range(1, HPG):
        p_sum = p_sum + p_c[:, hp * tq:(hp + 1) * tq]
    n_col = tq // LANES
    for c in range(n_col):
        psum_ref[c] = p_sum[:, c * LANES:(c + 1) * LANES]
    parts = [jnp.concatenate([psum_ref[c, pl.ds(o, ns, stride=per_blk), :] for c in range(n_col)], axis=1)
             for o in range(per_blk)]
    below = pltpu.roll(parts[per_blk - 1], 1, axis=0)
    below = jnp.where(lax.broadcasted_iota(jnp.int32, (ns, tq), 0) == 0, 0.0, below)
    imp = parts[0] + parts[1] + parts[2] + parts[3] + below
    nsp = selb_ref.shape[0]
    if nsp > ns:
        imp = jnp.concatenate([imp, jnp.zeros((nsp - ns, tq), F32)], axis=0)

    blk = lax.broadcasted_iota(jnp.int32, (nsp, tq), 0).astype(F32)
    r_lane = lax.broadcasted_iota(jnp.int32, (nsp, tq), 1)
    cur = ((tq // SEL_BLOCK) * qb + jnp.right_shift(r_lane, SEL_BLOCK.bit_length() - 1)).astype(F32)
    forced = (blk == 0.0) | (blk == cur) | (blk == cur - 1.0)
    valid = blk <= cur
    score = jnp.where(valid & jnp.logical_not(forced), imp, -jnp.inf)
    sel = forced
    for _ in range(SEL_TOPK - 3):
        mx = jnp.max(score, axis=0, keepdims=True)
        first = jnp.min(jnp.where(score == mx, blk, 1e9), axis=0, keepdims=True)
        hit = blk == first
        sel = sel | hit
        score = jnp.where(hit, -jnp.inf, score)
    selb_ref[...] = jnp.where(sel & valid, 0.0, NEG).astype(BF16)


def _nsa_select(zt, gt, kc, vct, u_tab, bsz, seq):
    tq = u_tab.shape[2]
    nq = seq // tq
    nch = seq // CMP_STRIDE
    gw = HPG * HEAD_DIM
    n = bsz * seq
    sel_rows = HEAD_DIM
    assert seq // SEL_BLOCK <= sel_rows
    return pl.pallas_call(
        _nsa_select_kernel,
        name="nsa_select",
        grid=(bsz, NSA_GROUPS, nq),
        in_specs=[pl.BlockSpec((gw, tq), lambda b, g, q: (ROW_QA // gw + g, b * nq + q)),
                  pl.BlockSpec((ROWS_SMALL, tq), lambda b, g, q: (0, b * nq + q)),
                  pl.BlockSpec((1, 1, nch, LANES), lambda b, g, q: (b, g, 0, 0)),
                  pl.BlockSpec((1, 1, HEAD_DIM, nch), lambda b, g, q: (b, g, 0, 0)),
                  pl.BlockSpec((HPG,) + u_tab.shape[1:], lambda b, g, q: (g, 0, 0))],
        out_specs=[pl.BlockSpec((sel_rows, tq), lambda b, g, q: (g, b * nq + q)),
                   pl.BlockSpec((gw, tq), lambda b, g, q: (g, b * nq + q))],
        out_shape=[jax.ShapeDtypeStruct((NSA_GROUPS * sel_rows, n), BF16),
                   jax.ShapeDtypeStruct((NSA_HEADS * HEAD_DIM, n), F32)],
        scratch_shapes=[pltpu.VMEM((nch, HPG * tq), F32), pltpu.VMEM((tq // LANES, nch, LANES), F32)],
        compiler_params=_params(("arbitrary", "arbitrary", "arbitrary")),
    )(zt, gt, kc, vct, u_tab)


def _nsa_kernel(q_ref, gate_ref, selb_ref, partc_ref, ks_ref, vst_ref, kw_ref, vwt_ref, oh_ref,
                bn_ref, band_ref, o_ref, sa_ref, sb_ref, *sw_refs):
    g = pl.program_id(1)
    qb = pl.program_id(2)
    seq = ks_ref.shape[0]
    tq = NSA_TQ
    wide = HPG * tq

    def lanes4(fn):
        return jnp.concatenate([fn(hp) for hp in range(HPG)], axis=1)

    qs = [q_ref[hp * HEAD_DIM:(hp + 1) * HEAD_DIM, :] for hp in range(HPG)]
    one_row = (lax.broadcasted_iota(jnp.int32, (HEAD_DIM, tq), 0) == 0).astype(BF16)
    q_pad = lanes4(lambda hp: jnp.concatenate([qs[hp], one_row], axis=0))
    key_bias_lane = (lax.broadcasted_iota(jnp.int32, (1, LANES), 1) == HEAD_DIM).astype(F32)
    bn_prev = lanes4(lambda hp: bn_ref[hp, 0:tq, :])
    bn_diag = lanes4(lambda hp: bn_ref[hp, tq:2 * tq, :])

    n_win = NSA_WINDOW // tq
    band4 = lanes4(lambda hp: band_ref[...])
    win_start = []
    for a in range(n_win + 1):
        ti = qb - n_win + a
        t0 = pl.multiple_of(jnp.maximum(ti, 0) * tq, tq)
        win_start.append(t0)
        k_a = kw_ref[pl.ds(t0, tq), :]
        if a < n_win:
            k_a = k_a + (key_bias_lane * jnp.where(ti >= 0, 0.0, NEG)).astype(BF16)
        s_a = _dot(k_a, q_pad)
        if a == 0:
            s_a = s_a + band4
        if a == n_win - 1:
            s_a = s_a + bn_prev
        if a == n_win:
            s_a = s_a + bn_diag
        sw_refs[a][...] = s_a

    selb = selb_ref[...]
    blk = lax.broadcasted_iota(jnp.int32, selb.shape, 0)
    selb_far = jnp.where(blk < (qb - 1) * (tq // SEL_BLOCK), selb, jnp.asarray(NEG, BF16))
    q_near = lanes4(lambda hp: jnp.concatenate([qs[hp], selb], axis=0))
    q_far = lanes4(lambda hp: jnp.concatenate([qs[hp], selb_far], axis=0))

    p0 = pl.multiple_of(jnp.maximum(qb - 1, 0) * tq, tq)
    d0 = pl.multiple_of(qb * tq, tq)
    p0_ids = pl.multiple_of(jnp.where(qb > 0, p0, seq - tq), tq)
    kp = ks_ref[pl.ds(p0, tq), :] + oh_ref[pl.ds(p0_ids, tq), :]
    kd = ks_ref[pl.ds(d0, tq), :] + oh_ref[pl.ds(d0, tq), :]
    sn_prev, sn_diag = sw_refs[n_win + 1], sw_refs[n_win + 2]
    sn_prev[...] = _dot(kp, q_near) + bn_prev
    sn_diag[...] = _dot(kd, q_near) + bn_diag

    def far_scores(tile, s_ref):
        k0 = pl.multiple_of(jnp.minimum(tile * SEL_TK, seq - SEL_TK), SEL_TK)
        s_ref[...] = _dot(ks_ref[pl.ds(k0, SEL_TK), :] + oh_ref[pl.ds(k0, SEL_TK), :], q_far)

    far_scores(0, sa_ref)

    carry_w = _softmax_init(wide)
    for a in range(n_win + 1):
        carry_w = _softmax_update(sw_refs[a][...], *carry_w, vwt_ref[:, pl.ds(win_start[a], tq)])
    o_w = _softmax_finish(carry_w[1])

    carry = _softmax_update(sn_prev[...], *_softmax_init(wide), vst_ref[:, pl.ds(p0, tq)])
    carry = _softmax_update(sn_diag[...], *carry, vst_ref[:, pl.ds(d0, tq)])

    def far_consume(s_ref, tile, carry):
        k0 = pl.multiple_of(tile * SEL_TK, SEL_TK)
        return _softmax_update(s_ref[...], *carry, vst_ref[:, pl.ds(k0, SEL_TK)])

    def far_body(j, carry):
        far_scores(2 * j + 1, sb_ref)
        carry = far_consume(sa_ref, 2 * j, carry)
        far_scores(2 * j + 2, sa_ref)
        return far_consume(sb_ref, 2 * j + 1, carry)

    far_tiles = 2 * SEL_TK // tq
    n_far = (jnp.maximum(qb - 1, 0) + far_tiles - 1) // far_tiles
    _, acc_s = lax.fori_loop(0, n_far, far_body, carry)
    o_s = _softmax_finish(acc_s)

    for hp in range(HPG):
        rows = slice(hp * HEAD_DIM, (hp + 1) * HEAD_DIM)
        sl = slice(hp * tq, (hp + 1) * tq)
        out = partc_ref[rows, :] + _gate_row(gate_ref, g, hp, 1) * o_s[:, sl] + _gate_row(gate_ref, g, hp, 2) * o_w[:, sl]
        o_ref[rows, :] = out.astype(BF16)


def _nsa(zt, zk, gt, selb, partc, onehot, bn_tab, band0, bsz, seq):
    nq = seq // NSA_TQ
    gw = HPG * HEAD_DIM
    n = bsz * seq
    nsp = selb.shape[0] // NSA_GROUPS
    const2 = lambda shp: pl.BlockSpec(shp, lambda b, g, q: (0, 0))
    return pl.pallas_call(
        _nsa_kernel,
        name="nsa_attn",
        grid=(bsz, NSA_GROUPS, nq),
        in_specs=[pl.BlockSpec((gw, NSA_TQ), lambda b, g, q: (ROW_QA // gw + g, b * nq + q)),
                  pl.BlockSpec((ROWS_SMALL, NSA_TQ), lambda b, g, q: (0, b * nq + q)),
                  pl.BlockSpec((nsp, NSA_TQ), lambda b, g, q: (g, b * nq + q)),
                  pl.BlockSpec((gw, NSA_TQ), lambda b, g, q: (g, b * nq + q)),
                  pl.BlockSpec((seq, LANES), lambda b, g, q: (b, CB_KSLC + g)),
                  pl.BlockSpec((HEAD_DIM, seq), lambda b, g, q: (ROW_VSLC // HEAD_DIM + g, b)),
                  pl.BlockSpec((seq, LANES), lambda b, g, q: (b, CB_KWIN + g)),
                  pl.BlockSpec((HEAD_DIM, seq), lambda b, g, q: (ROW_VWIN // HEAD_DIM + g, b)),
                  const2(onehot.shape),
                  pl.BlockSpec((HPG,) + bn_tab.shape[1:], lambda b, g, q: (g, 0, 0)),
                  const2(band0.shape)],
        out_specs=pl.BlockSpec((gw, NSA_TQ), lambda b, g, q: (g, b * nq + q)),
        out_shape=jax.ShapeDtypeStruct((NSA_HEADS * HEAD_DIM, n), BF16),
        scratch_shapes=[pltpu.VMEM((SEL_TK, HPG * NSA_TQ), F32), pltpu.VMEM((SEL_TK, HPG * NSA_TQ), F32)]
        + [pltpu.VMEM((NSA_TQ, HPG * NSA_TQ), F32) for _ in range(NSA_WINDOW // NSA_TQ + 3)],
        compiler_params=_params(("arbitrary", "arbitrary", "arbitrary")),
    )(zt, gt, selb, partc, zk, zt, zk, zt, onehot, bn_tab, band0)


SWA_SUB = 4


def _swa_kernel(layer, sink_ref, q_ref, kcur_ref, kprev_ref, vcur_ref, vprev_ref, bs_ref, o_ref, *s_refs):
    g = pl.program_id(1)
    t = pl.program_id(2)
    tq = SWA_WINDOW
    zeros_q = jnp.zeros((HEAD_DIM, tq), BF16)

    def lanes4(fn):
        return jnp.concatenate([fn(hp) for hp in range(HPG)], axis=1)

    bs_prev = lanes4(lambda hp: bs_ref[hp, 0:tq, :])
    bs_diag = lanes4(lambda hp: bs_ref[hp, tq:2 * tq, :])
    sink = lanes4(lambda hp: jnp.full((1, tq), sink_ref[layer, g * HPG + hp] * LOG2E, F32))
    for sub in range(SWA_SUB):
        cs = slice(sub * tq, (sub + 1) * tq)
        q_pad = lanes4(lambda hp: jnp.concatenate(
            [q_ref[hp * HEAD_DIM:(hp + 1) * HEAD_DIM, cs], zeros_q], axis=0))
        kp = kprev_ref[...] if sub == 0 else kcur_ref[(sub - 1) * tq:sub * tq, :]
        s_p = _dot(kp, q_pad) + bs_prev
        if sub == 0:
            s_p = jnp.where(t > 0, s_p, NEG)
        s_refs[2 * sub][...] = s_p
        s_refs[2 * sub + 1][...] = _dot(kcur_ref[cs, :], q_pad) + bs_diag
    init = (sink, jnp.concatenate([jnp.zeros((HEAD_DIM, HPG * tq), F32), jnp.ones((SUM_ROWS, HPG * tq), F32)], axis=0))
    for sub in range(SWA_SUB):
        cs = slice(sub * tq, (sub + 1) * tq)
        vp = vprev_ref[...] if sub == 0 else vcur_ref[:, (sub - 1) * tq:sub * tq]
        carry = _softmax_update(s_refs[2 * sub][...], *init, vp)
        _, acc = _softmax_update(s_refs[2 * sub + 1][...], *carry, vcur_ref[:, cs])
        o = _softmax_finish(acc)
        for hp in range(HPG):
            o_ref[hp * HEAD_DIM:(hp + 1) * HEAD_DIM, cs] = o[:, hp * tq:(hp + 1) * tq].astype(BF16)


def _swa(zt, zk, layer, sinks, bs_tab, bsz, seq):
    tq = SWA_WINDOW
    nq = seq // tq
    nt = nq // SWA_SUB
    wide = SWA_SUB * tq
    gw = HPG * HEAD_DIM
    n = bsz * seq
    prev = lambda b, t: b * nq + jnp.maximum(t * SWA_SUB - 1, 0)
    return pl.pallas_call(
        functools.partial(_swa_kernel, layer),
        name="swa_attn",
        grid=(bsz, SWA_GROUPS, nt),
        in_specs=[pl.BlockSpec(memory_space=pltpu.SMEM),
                  pl.BlockSpec((gw, wide), lambda b, g, t: (ROW_QB // gw + g, b * nt + t)),
                  pl.BlockSpec((wide, LANES), lambda b, g, t: (b * nt + t, CB_KB + g)),
                  pl.BlockSpec((tq, LANES), lambda b, g, t: (prev(b, t), CB_KB + g)),
                  pl.BlockSpec((HEAD_DIM, wide), lambda b, g, t: (ROW_VB // HEAD_DIM + g, b * nt + t)),
                  pl.BlockSpec((HEAD_DIM, tq), lambda b, g, t: (ROW_VB // HEAD_DIM + g, prev(b, t))),
                  pl.BlockSpec((HPG,) + bs_tab.shape[1:], lambda b, g, t: (g, 0, 0))],
        out_specs=pl.BlockSpec((gw, wide), lambda b, g, t: (g, b * nt + t)),
        out_shape=jax.ShapeDtypeStruct((SWA_HEADS * HEAD_DIM, n), BF16),
        scratch_shapes=[pltpu.VMEM((tq, HPG * tq), F32) for _ in range(2 * SWA_SUB)],
        compiler_params=_params(("arbitrary", "arbitrary", "arbitrary")),
    )(sinks, zt, zk, zk, zt, zt, bs_tab)


def _fox_keys_kernel(fz_ref, bf_ref, k_ref, o_ref, c_ref):
    seq = fz_ref.shape[0]
    ch = LANES
    logf = jax.nn.log_sigmoid(fz_ref[...] + bf_ref[...])
    tri = (lax.broadcasted_iota(jnp.int32, (ch, ch), 0) >= lax.broadcasted_iota(jnp.int32, (ch, ch), 1)).astype(F32)
    carry = jnp.zeros((1, LANES), F32)
    for i in range(seq // ch):
        c = jnp.dot(tri, logf[i * ch:(i + 1) * ch, :], preferred_element_type=F32,
                    precision=lax.Precision.HIGHEST) + carry
        c_ref[i * ch:(i + 1) * ch, :] = c
        carry = c[ch - 1:ch, :]
    neg = c_ref[...] * (-LOG2E)
    hi = neg.astype(BF16)
    r1 = neg - hi.astype(F32)
    mid = r1.astype(BF16)
    lo = (r1 - mid.astype(F32)).astype(BF16)
    parts = jnp.concatenate([hi, mid, lo], axis=1)
    wide = FOX_HEADS * LANES
    row = lax.broadcasted_iota(jnp.int32, (3 * LANES, wide), 0)
    col = lax.broadcasted_iota(jnp.int32, (3 * LANES, wide), 1)
    head, lane = jnp.right_shift(col, LANES.bit_length() - 1), jnp.bitwise_and(col, LANES - 1)
    term, src = jnp.right_shift(row, LANES.bit_length() - 1), jnp.bitwise_and(row, LANES - 1)
    place = (src == COL_FC + head) & (lane == LANE_C + term)
    bias = _dot(parts, place.astype(BF16)).astype(BF16)
    key_lane = jnp.bitwise_and(lax.broadcasted_iota(jnp.int32, (seq, wide), 1), LANES - 1)
    o_ref[...] = jnp.where(key_lane < HEAD_DIM, k_ref[...], bias)


def _fox_keys(fz, layer, bf_rows, zk, bsz, seq):
    wide = FOX_HEADS * LANES
    return pl.pallas_call(
        _fox_keys_kernel,
        name="fox_keys",
        grid=(bsz,),
        in_specs=[pl.BlockSpec((seq, LANES), lambda b: (b, 0)),
                  _layer_spec(bf_rows, layer),
                  pl.BlockSpec((seq, wide), lambda b: (b, CB_KC))],
        out_specs=pl.BlockSpec((seq, wide), lambda b: (b, 0)),
        out_shape=jax.ShapeDtypeStruct((bsz * seq, wide), BF16),
        scratch_shapes=[pltpu.VMEM((seq, LANES), F32)],
        compiler_params=_params(("arbitrary",)),
    )(fz, bf_rows, zk)


FOX_T = 512
FOX_HPS = 4


def _fox_kernel(q_ref, k_ref, vt_ref, o_ref, *s_refs):
    qi = pl.program_id(2)
    t = q_ref.shape[1]
    tk = t // 2
    ones_rows = (lax.broadcasted_iota(jnp.int32, (HEAD_DIM, t), 0) < 3).astype(BF16)
    rows = lambda hh: slice(hh * HEAD_DIM, (hh + 1) * HEAD_DIM)
    q_aug = [jnp.concatenate([q_ref[rows(hh), :], ones_rows], axis=0) for hh in range(FOX_HPS)]

    def scores(tile, slot):
        k0 = pl.multiple_of(tile * tk, tk)
        for hh in range(FOX_HPS):
            s_refs[2 * hh + slot][...] = _dot(k_ref[pl.ds(k0, tk), hh * LANES:(hh + 1) * LANES], q_aug[hh])

    def consume(slot, tile, carry, first_masked_query=None):
        k0 = pl.multiple_of(tile * tk, tk)
        out = []
        for hh in range(FOX_HPS):
            s = s_refs[2 * hh + slot][...]
            if first_masked_query is not None:
                key = lax.broadcasted_iota(jnp.int32, (tk, t), 0) + first_masked_query
                s = jnp.where(key > lax.broadcasted_iota(jnp.int32, (tk, t), 1), NEG, s)
            out.append(_softmax_update(s, *carry[hh], vt_ref[rows(hh), pl.ds(k0, tk)]))
        return tuple(out)

    def body(j, carry):
        scores(2 * j + 1, 1)
        carry = consume(0, 2 * j, carry)
        scores(2 * j + 2, 0)
        return consume(1, 2 * j + 1, carry)

    scores(0, 0)
    carry = lax.fori_loop(0, qi, body, tuple(_softmax_init(t) for _ in range(FOX_HPS)))
    scores(2 * qi + 1, 1)
    carry = consume(0, 2 * qi, carry, 0)
    carry = consume(1, 2 * qi + 1, carry, tk)
    for hh in range(FOX_HPS):
        o_ref[rows(hh), :] = _softmax_finish(carry[hh][1]).astype(BF16)


def _fox(zt, kaug, bsz, seq):
    t = min(FOX_T, seq)
    nq = seq // t
    n = bsz * seq
    hw = FOX_HPS * HEAD_DIM
    assert ROW_QC % hw == 0 and ROW_VC % hw == 0
    return pl.pallas_call(
        _fox_kernel,
        name="fox_attn",
        grid=(bsz, FOX_HEADS // FOX_HPS, nq),
        in_specs=[pl.BlockSpec((hw, t), lambda b, h, q: (ROW_QC // hw + h, b * nq + q)),
                  pl.BlockSpec((seq, FOX_HPS * LANES), lambda b, h, q: (b, h)),
                  pl.BlockSpec((hw, seq), lambda b, h, q: (ROW_VC // hw + h, b))],
        out_specs=pl.BlockSpec((hw, t), lambda b, h, q: (h, b * nq + q)),
        out_shape=jax.ShapeDtypeStruct((FOX_HEADS * HEAD_DIM, n), BF16),
        scratch_shapes=[pltpu.VMEM((t // 2, t), F32) for _ in range(2 * FOX_HPS)],
        compiler_params=_params(("arbitrary", "arbitrary", "arbitrary")),
    )(zt, kaug, zt)


MERGE_TM = 1024


def _merge_ln_kernel(x_ref, xb_ref, oa_ref, ob_ref, oc_ref, wg_ref, bg_ref, wbr_ref, wo_ref, g_ref, b_ref,
                     o_ref, obf_ref):
    d = x_ref.shape[1]
    xb = xb_ref[...]
    merged = None
    for i, br_ref in enumerate((oa_ref, ob_ref, oc_ref)):
        gate = jax.nn.sigmoid(_dot(xb, wg_ref[:, i * d:(i + 1) * d]) + bg_ref[:, i * d:(i + 1) * d])
        term = gate * _dot_tn(br_ref[...], wbr_ref[i])
        merged = term if merged is None else merged + term
    mix = _dot(merged.astype(BF16), wo_ref[...])
    out = _layer_norm(DN_ALPHA * x_ref[...] + mix, g_ref[...], b_ref[...])
    o_ref[...] = out
    obf_ref[...] = out.astype(BF16)


def _merge_ln(x, xb, ota, otb, otc, layer, wg, bg, wbr, wo, g, b):
    n, d = x.shape
    tm = min(MERGE_TM, n)
    w = ota.shape[0]
    tok = pl.BlockSpec((tm, d), lambda i: (i, 0))
    feat = pl.BlockSpec((w, tm), lambda i: (0, i))
    return pl.pallas_call(
        _merge_ln_kernel,
        name="merge_ln",
        grid=(n // tm,),
        in_specs=[tok, tok, feat, feat, feat]
        + [_layer_spec(a, layer, single_buffer=True) for a in (wg, bg, wbr, wo, g, b)],
        out_specs=[tok, tok],
        out_shape=[jax.ShapeDtypeStruct((n, d), F32), jax.ShapeDtypeStruct((n, d), BF16)],
        compiler_params=_params(("arbitrary",)),
    )(x, xb, ota, otb, otc, wg, bg, wbr, wo, g, b)


def _split_w_in(w_in):
    sizes = [NSA_HEADS * HEAD_DIM, 6 * NSA_GROUPS * HEAD_DIM, N_GA, SWA_HEADS * HEAD_DIM,
             SWA_GROUPS * HEAD_DIM, SWA_GROUPS * HEAD_DIM, FOX_HEADS * HEAD_DIM, FOX_HEADS * HEAD_DIM,
             FOX_HEADS * HEAD_DIM, FOX_HEADS]
    q_a, kv_a, g_a, q_b, k_b, v_b, q_c, k_c, v_c, f_c = jnp.split(w_in, np.cumsum(sizes)[:-1].tolist(), axis=-1)
    k_cmp, v_cmp, k_slc, v_slc, k_win, v_win = jnp.split(kv_a, 6, axis=-1)
    depth, d = w_in.shape[0], w_in.shape[1]

    def pad_heads(w):
        nh = w.shape[-1] // HEAD_DIM
        w = w.reshape(depth, d, nh, HEAD_DIM)
        return jnp.concatenate([w, jnp.zeros_like(w)], axis=-1).reshape(depth, d, nh * LANES)

    w_feat_t = jnp.swapaxes(jnp.concatenate([q_a, q_b, q_c, v_c, v_slc, v_win, v_b], axis=-1), 1, 2)
    w_k = jnp.concatenate([pad_heads(k_c), pad_heads(k_slc), pad_heads(k_win), pad_heads(k_b)], axis=-1)
    w_cmp = jnp.concatenate([k_cmp, v_cmp], axis=-1)
    small = jnp.concatenate([g_a, f_c], axis=-1)
    w_small = jnp.concatenate([small, jnp.zeros((depth, d, LANES - ROWS_SMALL), small.dtype)], axis=-1)
    w_small_t = jnp.swapaxes(small, 1, 2)
    return w_feat_t, w_k, w_cmp, w_small, w_small_t


def kernel(x, rel_bias, ln1_g, ln1_b, ffn1_w1, ffn1_w2, w_in, cmp_pe_k, cmp_pe_v, cmp_k_w1, cmp_k_w2, cmp_v_w1,
           cmp_v_w2, swa_sinks, fox_b_f, w_br_a, w_br_b, w_br_c, w_gate, b_gate, w_out, ln2_g, ln2_b, ffn2_w1,
           ffn2_w2, ln3_g, ln3_b):
    bsz, seq, d = x.shape
    n = bsz * seq
    depth = w_in.shape[0]
    nch = seq // CMP_STRIDE
    tabs = _static_tables(seq)
    onehot = jnp.asarray(tabs["onehot"], BF16)
    band0 = jnp.asarray(tabs["band0"])
    u_tab, bn_tab, bs_tab = _bias_tables(rel_bias, tabs)

    bf = lambda w: w.astype(BF16)
    w_feat_t, w_k, w_cmp, w_small, w_small_t = [bf(w) for w in _split_w_in(w_in)]
    f1w1, f1w2, f2w1, f2w2 = bf(ffn1_w1), bf(ffn1_w2), bf(ffn2_w1), bf(ffn2_w2)
    ck1, ck2, cv1 = bf(cmp_k_w1), bf(cmp_k_w2), bf(cmp_v_w1)
    cv2t = bf(jnp.swapaxes(cmp_v_w2, 1, 2))
    pek = cmp_pe_k.reshape(depth, 1, CMP_BLOCK * HEAD_DIM)
    pev = cmp_pe_v.reshape(depth, 1, CMP_BLOCK * HEAD_DIM)
    wg, wo = bf(w_gate), bf(w_out)
    wbr = bf(jnp.stack([w_br_a, w_br_b, w_br_c], axis=1))
    bf_rows = jnp.zeros((depth, 1, LANES), F32).at[:, 0, COL_FC:COL_FC + FOX_HEADS].set(fox_b_f)
    rows = lambda v: v.reshape(depth, 1, -1)
    ln1, ln2, ln3 = (rows(ln1_g), rows(ln1_b)), (rows(ln2_g), rows(ln2_b)), (rows(ln3_g), rows(ln3_b))
    bg = rows(b_gate)

    h = x.reshape(n, d)
    for l in range(depth):
        h, hb = _ffn_ln(h, l, f1w1, f1w2, *ln1)
        zt, zk, zc, fz, gt = _project(hb, l, w_feat_t, w_k, w_cmp, w_small, w_small_t)
        zc4 = zc.reshape(bsz, nch, CMP_STRIDE, 2 * NSA_GROUPS, HEAD_DIM).transpose(0, 3, 1, 2, 4)
        zc4 = zc4.reshape(bsz, 2 * NSA_GROUPS, nch, CMP_STRIDE * HEAD_DIM)
        kc, vct = _compress(zc4, l, pek, pev, ck1, ck2, cv1, cv2t)
        selb, part_c = _nsa_select(zt, gt, kc, vct, u_tab, bsz, seq)
        ot_a = _nsa(zt, zk, gt, selb, part_c, onehot, bn_tab, band0, bsz, seq)
        ot_b = _swa(zt, zk, l, swa_sinks, bs_tab, bsz, seq)
        kaug = _fox_keys(fz, l, bf_rows, zk, bsz, seq)
        ot_c = _fox(zt, kaug, bsz, seq)
        h, hb = _merge_ln(h, hb, ot_a, ot_b, ot_c, l, wg, bg, wbr, wo, *ln2)
        h, hb = _ffn_ln(h, l, f2w1, f2w2, *ln3)
    return h.reshape(bsz, seq, d)
```

```python
import functools
import math

import numpy as np
import jax
import jax.numpy as jnp
from jax import lax
from jax.experimental import pallas as pl
from jax.experimental.pallas import tpu as pltpu

F32 = jnp.float32
BF16 = jnp.bfloat16

D_MODEL = 1024
DEPTH = 4
HEAD_DIM = 64
NSA_HEADS = 8
NSA_GROUPS = 2
HPG = NSA_HEADS // NSA_GROUPS
CMP_BLOCK = 32
CMP_STRIDE = 16
CMP_HIDDEN = 256
SEL_BLOCK = 64
SEL_TOPK = 8
NSA_WINDOW = 512
SWA_HEADS = 8
SWA_GROUPS = 2
SWA_WINDOW = 128
FOX_HEADS = 8
NSA_TQ = 256
NSA_SEL_TQ = 1024
REL_BUCKETS = 32
REL_MAX_DIST = 128
D_FF = 2816
LN_EPS = 1e-5
DN_ALPHA = (2 * DEPTH) ** 0.25
SCALE = HEAD_DIM ** -0.5
LOG2E = math.log2(math.e)
Q_SCALE = SCALE * LOG2E
SUM_ROWS = 16

NEG = -1e30
NEG_TEST = -1e29
LANES = 128
VMEM_LIMIT = 56 * 1024 * 1024

ROW_QA, ROW_QB, ROW_QC = 0, 512, 1024
ROW_VC, ROW_VSLC, ROW_VWIN, ROW_VB = 1536, 2048, 2176, 2304
ROWS_Q = ROW_VC
ROWS_FEAT = 2432
CB_KC, CB_KSLC, CB_KWIN, CB_KB = 0, 8, 10, 12
COLS_K = 14 * LANES
N_GA = 3 * NSA_HEADS
COL_FC = N_GA
ROWS_SMALL = 32
LANE_C = HEAD_DIM


def _params(sem, vmem=VMEM_LIMIT):
    return pltpu.CompilerParams(dimension_semantics=sem, vmem_limit_bytes=vmem)


def _layer_spec(w, layer, single_buffer=False):
    zeros = (0,) * (w.ndim - 1)
    mode = dict(pipeline_mode=pl.Buffered(1)) if single_buffer else {}
    return pl.BlockSpec((None,) + w.shape[1:], lambda *_: (layer,) + zeros, **mode)


def _dot(a, b):
    return jnp.dot(a, b, preferred_element_type=F32)


def _dot_nt(a, b):
    return lax.dot_general(a, b, (((1,), (1,)), ((), ())), preferred_element_type=F32)


def _dot_tn(a, b):
    return lax.dot_general(a, b, (((0,), (0,)), ((), ())), preferred_element_type=F32)


def _layer_norm(y, g, b):
    mu = jnp.mean(y, axis=-1, keepdims=True)
    yc = y - mu
    var = jnp.mean(yc * yc, axis=-1, keepdims=True)
    return yc * lax.rsqrt(var + LN_EPS) * g + b


def _bucket_np(n):
    n = np.maximum(n, 0)
    exact = REL_BUCKETS // 2
    ratio = np.log(np.maximum(n, 1).astype(np.float64) / exact) / math.log(REL_MAX_DIST / exact)
    large = exact + (ratio * (REL_BUCKETS - exact)).astype(np.int64)
    return np.where(n < exact, n, np.minimum(large, REL_BUCKETS - 1)).astype(np.int32)


def _static_tables(seq):
    nch = seq // CMP_STRIDE
    ns = seq // SEL_BLOCK

    def near_buckets(tq):
        r = np.arange(tq)[None, :]
        c = np.arange(tq)[:, None]
        d_prev = tq + r - c
        d_diag = r - c
        return d_prev, _bucket_np(d_prev), np.where(d_diag >= 0, _bucket_np(d_diag), -1)

    r = np.arange(NSA_TQ)[None, :]
    r_sel = np.arange(min(NSA_SEL_TQ, seq))[None, :]
    u = np.arange(2 * nch)[:, None]
    n_c = r_sel - (CMP_BLOCK - 1) - CMP_STRIDE * (u - nch)
    bk_c = np.where(n_c >= 0, _bucket_np(n_c), -1).astype(np.int32)
    _, bk_prev, bk_diag = near_buckets(NSA_TQ)
    bk_near = np.concatenate([bk_prev, bk_diag], axis=0).astype(np.int32)
    d_prev, bk_prev, bk_diag = near_buckets(SWA_WINDOW)
    bk_swa = np.concatenate([np.where(d_prev < SWA_WINDOW, bk_prev, -1), bk_diag], axis=0).astype(np.int32)
    band0 = np.where(np.arange(NSA_TQ)[:, None] > r, 0.0, NEG).astype(np.float32)
    s = np.arange(seq)[:, None]
    lane = np.arange(LANES)[None, :]
    onehot = (lane == HEAD_DIM + s // SEL_BLOCK).astype(np.float32)
    return dict(bk_c=bk_c, bk_near=bk_near, bk_swa=bk_swa, band0=band0, onehot=onehot)


def _tables_kernel(rel_ref, bkc_ref, bkn_ref, bks_ref, u_ref, bn_ref, bs_ref):
    h = pl.program_id(0)

    def build(bk, col, delta):
        base = rel_ref[REL_BUCKETS - 1, col] if delta else 0.0
        acc = jnp.zeros(bk.shape, F32)
        for b in range(REL_BUCKETS):
            acc = jnp.where(bk == b, (rel_ref[b, col] - base) * LOG2E, acc)
        return jnp.where(bk < 0, NEG, acc)

    u_ref[0] = build(bkc_ref[...], h, True)
    bn_ref[0] = build(bkn_ref[...], h, True)
    bs_ref[0] = build(bks_ref[...], NSA_HEADS + h, False)


def _bias_tables(rel_bias, tabs):
    buckets = [jnp.asarray(tabs[k]) for k in ("bk_c", "bk_near", "bk_swa")]
    full = lambda a: pl.BlockSpec(a.shape, lambda h: (0, 0))
    per_head = lambda a: pl.BlockSpec((1,) + a.shape, lambda h: (h, 0, 0))
    return pl.pallas_call(
        _tables_kernel,
        name="bias_tables",
        grid=(NSA_HEADS,),
        in_specs=[pl.BlockSpec(memory_space=pltpu.SMEM)] + [full(a) for a in buckets],
        out_specs=[per_head(a) for a in buckets],
        out_shape=[jax.ShapeDtypeStruct((NSA_HEADS,) + a.shape, F32) for a in buckets],
        compiler_params=_params(("arbitrary",)),
    )(rel_bias, *buckets)


FFN_TM = 1024
FFN_TF = 256


def _ffn_ln_kernel(x_ref, w1_ref, w2_ref, g_ref, b_ref, o_ref, ob_ref, act_ref):
    f = w2_ref.shape[0]
    x = x_ref[...]
    xb = x.astype(BF16)
    for c in range(f // FFN_TF):
        gt = _dot(xb, w1_ref[:, c * FFN_TF:(c + 1) * FFN_TF])
        up = _dot(xb, w1_ref[:, f + c * FFN_TF:f + (c + 1) * FFN_TF])
        act_ref[:, c * FFN_TF:(c + 1) * FFN_TF] = (gt * jax.nn.sigmoid(gt) * up).astype(BF16)
    y = DN_ALPHA * x + 0.5 * _dot(act_ref[...], w2_ref[...])
    out = _layer_norm(y, g_ref[...], b_ref[...])
    o_ref[...] = out
    ob_ref[...] = out.astype(BF16)


def _ffn_ln(x, layer, w1, w2, g, b):
    n, d = x.shape
    f = w2.shape[1]
    tm = min(FFN_TM, n)
    resident = lambda a: _layer_spec(a, layer, single_buffer=True)
    return pl.pallas_call(
        _ffn_ln_kernel,
        name="ffn_ln",
        grid=(n // tm,),
        in_specs=[pl.BlockSpec((tm, d), lambda i: (i, 0)), resident(w1), resident(w2), resident(g), resident(b)],
        out_specs=[pl.BlockSpec((tm, d), lambda i: (i, 0)), pl.BlockSpec((tm, d), lambda i: (i, 0))],
        out_shape=[jax.ShapeDtypeStruct((n, d), F32), jax.ShapeDtypeStruct((n, d), BF16)],
        scratch_shapes=[pltpu.VMEM((tm, f), BF16)],
        compiler_params=_params(("arbitrary",)),
    )(x, w1, w2, g, b)


PROJ_TM = 1024


def _proj_kernel(xb_ref, wft_ref, wk_ref, wc_ref, ws_ref, wst_ref, zt_ref, zk_ref, zc_ref, fz_ref, gt_ref):
    xb = xb_ref[...]
    zt_ref[:ROWS_Q, :] = (_dot_nt(wft_ref[:ROWS_Q, :], xb) * Q_SCALE).astype(BF16)
    zt_ref[ROWS_Q:, :] = _dot_nt(wft_ref[ROWS_Q:, :], xb).astype(BF16)
    keys = _dot(xb, wk_ref[...])
    zero = jnp.zeros((keys.shape[0], LANES - HEAD_DIM), F32)
    pieces = [p for h in range(keys.shape[1] // HEAD_DIM) for p in (keys[:, h * HEAD_DIM:(h + 1) * HEAD_DIM], zero)]
    zk_ref[...] = jnp.concatenate(pieces, axis=1).astype(BF16)
    zc_ref[...] = _dot(xb, wc_ref[...]).astype(BF16)
    fz_ref[...] = _dot(xb, ws_ref[...])
    gt_ref[...] = _dot_nt(wst_ref[...], xb)


def _project(xb, layer, wft, wk, wc, ws, wst):
    n, d = xb.shape
    tm = min(PROJ_TM, n)
    return pl.pallas_call(
        _proj_kernel,
        name="proj_in",
        grid=(n // tm,),
        in_specs=[pl.BlockSpec((tm, d), lambda i: (i, 0))] + [_layer_spec(w, layer) for w in (wft, wk, wc, ws, wst)],
        out_specs=[pl.BlockSpec((ROWS_FEAT, tm), lambda i: (0, i)),
                   pl.BlockSpec((tm, COLS_K), lambda i: (i, 0)),
                   pl.BlockSpec((tm, 4 * HEAD_DIM), lambda i: (i, 0)),
                   pl.BlockSpec((tm, LANES), lambda i: (i, 0)),
                   pl.BlockSpec((ROWS_SMALL, tm), lambda i: (0, i))],
        out_shape=[jax.ShapeDtypeStruct((ROWS_FEAT, n), BF16),
                   jax.ShapeDtypeStruct((n, COLS_K), BF16),
                   jax.ShapeDtypeStruct((n, 4 * HEAD_DIM), BF16),
                   jax.ShapeDtypeStruct((n, LANES), F32),
                   jax.ShapeDtypeStruct((ROWS_SMALL, n), F32)],
        compiler_params=_params(("arbitrary",)),
    )(xb, wft, wk, wc, ws, wst)


def _gelu_tanh(x):
    return 0.5 * x * (1.0 + jnp.tanh(math.sqrt(2.0 / math.pi) * (x + 0.044715 * (x * x * x))))


def _compress_kernel(ck_ref, cv_ref, pek_ref, pev_ref, wk1_ref, wk2_ref, wv1_ref, wv2t_ref, kc_ref, vct_ref):
    half = CMP_STRIDE * HEAD_DIM
    nch = ck_ref.shape[2]

    def hidden(c_ref, pe_ref, w1_ref):
        c = c_ref[0, 0].astype(F32)
        top = (c + pe_ref[:, :half]).astype(BF16)
        bot = (c + pe_ref[:, half:]).astype(BF16)
        a = _dot(top, w1_ref[:half, :])
        bm = _dot(bot, w1_ref[half:, :])
        h = a + pltpu.roll(bm, nch - 1, axis=0)
        return _gelu_tanh(h).astype(BF16)

    kc = _dot(hidden(ck_ref, pek_ref, wk1_ref), wk2_ref[...])
    kc_ref[0, 0] = jnp.concatenate([kc, jnp.zeros_like(kc)], axis=1).astype(BF16)
    vct_ref[0, 0] = _dot_nt(wv2t_ref[...], hidden(cv_ref, pev_ref, wv1_ref)).astype(BF16)


def _compress(zc4, layer, pek, pev, wk1, wk2, wv1, wv2t):
    bsz, _, nch, wide = zc4.shape
    return pl.pallas_call(
        _compress_kernel,
        name="nsa_compress",
        grid=(bsz, NSA_GROUPS),
        in_specs=[pl.BlockSpec((1, 1, nch, wide), lambda b, g: (b, g, 0, 0)),
                  pl.BlockSpec((1, 1, nch, wide), lambda b, g: (b, NSA_GROUPS + g, 0, 0))]
        + [_layer_spec(w, layer) for w in (pek, pev, wk1, wk2, wv1, wv2t)],
        out_specs=[pl.BlockSpec((1, 1, nch, LANES), lambda b, g: (b, g, 0, 0)),
                   pl.BlockSpec((1, 1, HEAD_DIM, nch), lambda b, g: (b, g, 0, 0))],
        out_shape=[jax.ShapeDtypeStruct((bsz, NSA_GROUPS, nch, LANES), BF16),
                   jax.ShapeDtypeStruct((bsz, NSA_GROUPS, HEAD_DIM, nch), BF16)],
        compiler_params=_params(("arbitrary", "arbitrary")),
    )(zc4, zc4, pek, pev, wk1, wk2, wv1, wv2t)


SEL_TK = 256


def _with_sum_rows(vt):
    return jnp.concatenate([vt, jnp.ones((SUM_ROWS, vt.shape[1]), vt.dtype)], axis=0)


def _softmax_update(s, m, acc, vt):
    mn = jnp.maximum(m, jnp.max(s, axis=0, keepdims=True))
    p = jnp.exp2(s - mn).astype(BF16)
    acc = jnp.exp2(m - mn) * acc + _dot(_with_sum_rows(vt), p)
    return mn, acc


def _softmax_init(width):
    return jnp.full((1, width), -jnp.inf, F32), jnp.zeros((HEAD_DIM + SUM_ROWS, width), F32)


def _softmax_finish(acc):
    return acc[:HEAD_DIM] * (1.0 / acc[HEAD_DIM:HEAD_DIM + 1])


def _gate_row(gate_ref, g, hp, branch):
    return jax.nn.sigmoid(gate_ref[pl.ds(g * (HPG * 3) + hp * 3 + branch, 1), :])


def _nsa_select_kernel(q_ref, gate_ref, kc_ref, vct_ref, u_ref, selb_ref, part_ref, sc_ref, psum_ref):
    g = pl.program_id(1)
    qb = pl.program_id(2)
    nch = kc_ref.shape[2]
    tq = q_ref.shape[1]

    def lanes4(fn):
        return jnp.concatenate([fn(hp) for hp in range(HPG)], axis=1)

    zeros_q = jnp.zeros((HEAD_DIM, tq), BF16)
    q_pad = lanes4(lambda hp: jnp.concatenate([q_ref[hp * HEAD_DIM:(hp + 1) * HEAD_DIM, :], zeros_q], axis=0))

    off = pl.multiple_of(nch - (tq // CMP_STRIDE) * qb, 8)
    sc_ref[...] = _dot(kc_ref[0, 0], q_pad) + lanes4(lambda hp: u_ref[hp, pl.ds(off, nch), :])
    s_c = sc_ref[...]
    m_c = jnp.max(s_c, axis=0, keepdims=True)
    p_c = jnp.exp2(s_c - m_c)
    l_c = jnp.sum(p_c, axis=0, keepdims=True)
    p_c = p_c * jnp.where(m_c > NEG_TEST, 1.0 / jnp.maximum(l_c, 1e-30), 0.0)
    o_c = _dot(vct_ref[0, 0], p_c.astype(BF16))
    for hp in range(HPG):
        part_ref[hp * HEAD_DIM:(hp + 1) * HEAD_DIM, :] = _gate_row(gate_ref, g, hp, 0) * o_c[:, hp * tq:(hp + 1) * tq]

    per_blk = SEL_BLOCK // CMP_STRIDE
    ns = nch // per_blk
    p_sum = p_c[:, 0:tq]
    for hp in range(1, HPG):
        p_sum = p_sum + p_c[:, hp * tq:(hp + 1) * tq]
    n_col = tq // LANES
    for c in range(n_col):
        psum_ref[c] = p_sum[:, c * LANES:(c + 1) * LANES]
    parts = [jnp.concatenate([psum_ref[c, pl.ds(o, ns, stride=per_blk), :] for c in range(n_col)], axis=1)
             for o in range(per_blk)]
    below = pltpu.roll(parts[per_blk - 1], 1, axis=0)
    below = jnp.where(lax.broadcasted_iota(jnp.int32, (ns, tq), 0) == 0, 0.0, below)
    imp = parts[0] + parts[1] + parts[2] + parts[3] + below
    nsp = selb_ref.shape[0]
    if nsp > ns:
        imp = jnp.concatenate([imp, jnp.zeros((nsp - ns, tq), F32)], axis=0)

    blk = lax.broadcasted_iota(jnp.int32, (nsp, tq), 0).astype(F32)
    r_lane = lax.broadcasted_iota(jnp.int32, (nsp, tq), 1)
    cur = ((tq // SEL_BLOCK) * qb + jnp.right_shift(r_lane, SEL_BLOCK.bit_length() - 1)).astype(F32)
    forced = (blk == 0.0) | (blk == cur) | (blk == cur - 1.0)
    valid = blk <= cur
    score = jnp.where(valid & jnp.logical_not(forced), imp, -jnp.inf)
    sel = forced
    for _ in range(SEL_TOPK - 3):
        mx = jnp.max(score, axis=0, keepdims=True)
        first = jnp.min(jnp.where(score == mx, blk, 1e9), axis=0, keepdims=True)
        hit = blk == first
        sel = sel | hit
        score = jnp.where(hit, -jnp.inf, score)
    selb_ref[...] = jnp.where(sel & valid, 0.0, NEG).astype(BF16)


def _nsa_select(zt, gt, kc, vct, u_tab, bsz, seq):
    tq = u_tab.shape[2]
    nq = seq // tq
    nch = seq // CMP_STRIDE
    gw = HPG * HEAD_DIM
    n = bsz * seq
    sel_rows = HEAD_DIM
    assert seq // SEL_BLOCK <= sel_rows
    return pl.pallas_call(
        _nsa_select_kernel,
        name="nsa_select",
        grid=(bsz, NSA_GROUPS, nq),
        in_specs=[pl.BlockSpec((gw, tq), lambda b, g, q: (ROW_QA // gw + g, b * nq + q)),
                  pl.BlockSpec((ROWS_SMALL, tq), lambda b, g, q: (0, b * nq + q)),
                  pl.BlockSpec((1, 1, nch, LANES), lambda b, g, q: (b, g, 0, 0)),
                  pl.BlockSpec((1, 1, HEAD_DIM, nch), lambda b, g, q: (b, g, 0, 0)),
                  pl.BlockSpec((HPG,) + u_tab.shape[1:], lambda b, g, q: (g, 0, 0))],
        out_specs=[pl.BlockSpec((sel_rows, tq), lambda b, g, q: (g, b * nq + q)),
                   pl.BlockSpec((gw, tq), lambda b, g, q: (g, b * nq + q))],
        out_shape=[jax.ShapeDtypeStruct((NSA_GROUPS * sel_rows, n), BF16),
                   jax.ShapeDtypeStruct((NSA_HEADS * HEAD_DIM, n), F32)],
        scratch_shapes=[pltpu.VMEM((nch, HPG * tq), F32), pltpu.VMEM((tq // LANES, nch, LANES), F32)],
        compiler_params=_params(("arbitrary", "arbitrary", "arbitrary")),
    )(zt, gt, kc, vct, u_tab)


def _nsa_kernel(q_ref, gate_ref, selb_ref, partc_ref, ks_ref, vst_ref, kw_ref, vwt_ref, oh_ref,
                bn_ref, band_ref, o_ref, sa_ref, sb_ref, *sw_refs):
    g = pl.program_id(1)
    qb = pl.program_id(2)
    seq = ks_ref.shape[0]
    tq = NSA_TQ
    wide = HPG * tq

    def lanes4(fn):
        return jnp.concatenate([fn(hp) for hp in range(HPG)], axis=1)

    qs = [q_ref[hp * HEAD_DIM:(hp + 1) * HEAD_DIM, :] for hp in range(HPG)]
    one_row = (lax.broadcasted_iota(jnp.int32, (HEAD_DIM, tq), 0) == 0).astype(BF16)
    q_pad = lanes4(lambda hp: jnp.concatenate([qs[hp], one_row], axis=0))
    key_bias_lane = (lax.broadcasted_iota(jnp.int32, (1, LANES), 1) == HEAD_DIM).astype(F32)
    bn_prev = lanes4(lambda hp: bn_ref[hp, 0:tq, :])
    bn_diag = lanes4(lambda hp: bn_ref[hp, tq:2 * tq, :])

    n_win = NSA_WINDOW // tq
    band4 = lanes4(lambda hp: band_ref[...])
    win_start = []
    for a in range(n_win + 1):
        ti = qb - n_win + a
        t0 = pl.multiple_of(jnp.maximum(ti, 0) * tq, tq)
        win_start.append(t0)
        k_a = kw_ref[pl.ds(t0, tq), :]
        if a < n_win:
            k_a = k_a + (key_bias_lane * jnp.where(ti >= 0, 0.0, NEG)).astype(BF16)
        s_a = _dot(k_a, q_pad)
        if a == 0:
            s_a = s_a + band4
        if a == n_win - 1:
            s_a = s_a + bn_prev
        if a == n_win:
            s_a = s_a + bn_diag
        sw_refs[a][...] = s_a

    selb = selb_ref[...]
    blk = lax.broadcasted_iota(jnp.int32, selb.shape, 0)
    selb_far = jnp.where(blk < (qb - 1) * (tq // SEL_BLOCK), selb, jnp.asarray(NEG, BF16))
    q_near = lanes4(lambda hp: jnp.concatenate([qs[hp], selb], axis=0))
    q_far = lanes4(lambda hp: jnp.concatenate([qs[hp], selb_far], axis=0))

    p0 = pl.multiple_of(jnp.maximum(qb - 1, 0) * tq, tq)
    d0 = pl.multiple_of(qb * tq, tq)
    p0_ids = pl.multiple_of(jnp.where(qb > 0, p0, seq - tq), tq)
    kp = ks_ref[pl.ds(p0, tq), :] + oh_ref[pl.ds(p0_ids, tq), :]
    kd = ks_ref[pl.ds(d0, tq), :] + oh_ref[pl.ds(d0, tq), :]
    sn_prev, sn_diag = sw_refs[n_win + 1], sw_refs[n_win + 2]
    sn_prev[...] = _dot(kp, q_near) + bn_prev
    sn_diag[...] = _dot(kd, q_near) + bn_diag

    def far_scores(tile, s_ref):
        k0 = pl.multiple_of(jnp.minimum(tile * SEL_TK, seq - SEL_TK), SEL_TK)
        s_ref[...] = _dot(ks_ref[pl.ds(k0, SEL_TK), :] + oh_ref[pl.ds(k0, SEL_TK), :], q_far)

    far_scores(0, sa_ref)

    carry_w = _softmax_init(wide)
    for a in range(n_win + 1):
        carry_w = _softmax_update(sw_refs[a][...], *carry_w, vwt_ref[:, pl.ds(win_start[a], tq)])
    o_w = _softmax_finish(carry_w[1])

    carry = _softmax_update(sn_prev[...], *_softmax_init(wide), vst_ref[:, pl.ds(p0, tq)])
    carry = _softmax_update(sn_diag[...], *carry, vst_ref[:, pl.ds(d0, tq)])

    def far_consume(s_ref, tile, carry):
        k0 = pl.multiple_of(tile * SEL_TK, SEL_TK)
        return _softmax_update(s_ref[...], *carry, vst_ref[:, pl.ds(k0, SEL_TK)])

    def far_body(j, carry):
        far_scores(2 * j + 1, sb_ref)
        carry = far_consume(sa_ref, 2 * j, carry)
        far_scores(2 * j + 2, sa_ref)
        return far_consume(sb_ref, 2 * j + 1, carry)

    far_tiles = 2 * SEL_TK // tq
    n_far = (jnp.maximum(qb - 1, 0) + far_tiles - 1) // far_tiles
    _, acc_s = lax.fori_loop(0, n_far, far_body, carry)
    o_s = _softmax_finish(acc_s)

    for hp in range(HPG):
        rows = slice(hp * HEAD_DIM, (hp + 1) * HEAD_DIM)
        sl = slice(hp * tq, (hp + 1) * tq)
        out = partc_ref[rows, :] + _gate_row(gate_ref, g, hp, 1) * o_s[:, sl] + _gate_row(gate_ref, g, hp, 2) * o_w[:, sl]
        o_ref[rows, :] = out.astype(BF16)


def _nsa(zt, zk, gt, selb, partc, onehot, bn_tab, band0, bsz, seq):
    nq = seq // NSA_TQ
    gw = HPG * HEAD_DIM
    n = bsz * seq
    nsp = selb.shape[0] // NSA_GROUPS
    const2 = lambda shp: pl.BlockSpec(shp, lambda b, g, q: (0, 0))
    return pl.pallas_call(
        _nsa_kernel,
        name="nsa_attn",
        grid=(bsz, NSA_GROUPS, nq),
        in_specs=[pl.BlockSpec((gw, NSA_TQ), lambda b, g, q: (ROW_QA // gw + g, b * nq + q)),
                  pl.BlockSpec((ROWS_SMALL, NSA_TQ), lambda b, g, q: (0, b * nq + q)),
                  pl.BlockSpec((nsp, NSA_TQ), lambda b, g, q: (g, b * nq + q)),
                  pl.BlockSpec((gw, NSA_TQ), lambda b, g, q: (g, b * nq + q)),
                  pl.BlockSpec((seq, LANES), lambda b, g, q: (b, CB_KSLC + g)),
                  pl.BlockSpec((HEAD_DIM, seq), lambda b, g, q: (ROW_VSLC // HEAD_DIM + g, b)),
                  pl.BlockSpec((seq, LANES), lambda b, g, q: (b, CB_KWIN + g)),
                  pl.BlockSpec((HEAD_DIM, seq), lambda b, g, q: (ROW_VWIN // HEAD_DIM + g, b)),
                  const2(onehot.shape),
                  pl.BlockSpec((HPG,) + bn_tab.shape[1:], lambda b, g, q: (g, 0, 0)),
                  const2(band0.shape)],
        out_specs=pl.BlockSpec((gw, NSA_TQ), lambda b, g, q: (g, b * nq + q)),
        out_shape=jax.ShapeDtypeStruct((NSA_HEADS * HEAD_DIM, n), BF16),
        scratch_shapes=[pltpu.VMEM((SEL_TK, HPG * NSA_TQ), F32), pltpu.VMEM((SEL_TK, HPG * NSA_TQ), F32)]
        + [pltpu.VMEM((NSA_TQ, HPG * NSA_TQ), F32) for _ in range(NSA_WINDOW // NSA_TQ + 3)],
        compiler_params=_params(("arbitrary", "arbitrary", "arbitrary")),
    )(zt, gt, selb, partc, zk, zt, zk, zt, onehot, bn_tab, band0)


SWA_SUB = 8


def _swa_kernel(layer, sink_ref, q_ref, kcur_ref, kprev_ref, vcur_ref, vprev_ref, bs_ref, o_ref, *s_refs):
    g = pl.program_id(1)
    t = pl.program_id(2)
    tq = SWA_WINDOW
    zeros_q = jnp.zeros((HEAD_DIM, tq), BF16)

    def lanes4(fn):
        return jnp.concatenate([fn(hp) for hp in range(HPG)], axis=1)

    bs_prev = lanes4(lambda hp: bs_ref[hp, 0:tq, :])
    bs_diag = lanes4(lambda hp: bs_ref[hp, tq:2 * tq, :])
    sink = lanes4(lambda hp: jnp.full((1, tq), sink_ref[layer, g * HPG + hp] * LOG2E, F32))
    for sub in range(SWA_SUB):
        cs = slice(sub * tq, (sub + 1) * tq)
        q_pad = lanes4(lambda hp: jnp.concatenate(
            [q_ref[hp * HEAD_DIM:(hp + 1) * HEAD_DIM, cs], zeros_q], axis=0))
        kp = kprev_ref[...] if sub == 0 else kcur_ref[(sub - 1) * tq:sub * tq, :]
        s_p = _dot(kp, q_pad) + bs_prev
        if sub == 0:
            s_p = jnp.where(t > 0, s_p, NEG)
        s_refs[2 * sub][...] = s_p
        s_refs[2 * sub + 1][...] = _dot(kcur_ref[cs, :], q_pad) + bs_diag
    init = (sink, jnp.concatenate([jnp.zeros((HEAD_DIM, HPG * tq), F32), jnp.ones((SUM_ROWS, HPG * tq), F32)], axis=0))
    for sub in range(SWA_SUB):
        cs = slice(sub * tq, (sub + 1) * tq)
        vp = vprev_ref[...] if sub == 0 else vcur_ref[:, (sub - 1) * tq:sub * tq]
        carry = _softmax_update(s_refs[2 * sub][...], *init, vp)
        _, acc = _softmax_update(s_refs[2 * sub + 1][...], *carry, vcur_ref[:, cs])
        o = _softmax_finish(acc)
        for hp in range(HPG):
            o_ref[hp * HEAD_DIM:(hp + 1) * HEAD_DIM, cs] = o[:, hp * tq:(hp + 1) * tq].astype(BF16)


def _swa(zt, zk, layer, sinks, bs_tab, bsz, seq):
    tq = SWA_WINDOW
    nq = seq // tq
    nt = nq // SWA_SUB
    wide = SWA_SUB * tq
    gw = HPG * HEAD_DIM
    n = bsz * seq
    prev = lambda b, t: b * nq + jnp.maximum(t * SWA_SUB - 1, 0)
    return pl.pallas_call(
        functools.partial(_swa_kernel, layer),
        name="swa_attn",
        grid=(bsz, SWA_GROUPS, nt),
        in_specs=[pl.BlockSpec(memory_space=pltpu.SMEM),
                  pl.BlockSpec((gw, wide), lambda b, g, t: (ROW_QB // gw + g, b * nt + t)),
                  pl.BlockSpec((wide, LANES), lambda b, g, t: (b * nt + t, CB_KB + g)),
                  pl.BlockSpec((tq, LANES), lambda b, g, t: (prev(b, t), CB_KB + g)),
                  pl.BlockSpec((HEAD_DIM, wide), lambda b, g, t: (ROW_VB // HEAD_DIM + g, b * nt + t)),
                  pl.BlockSpec((HEAD_DIM, tq), lambda b, g, t: (ROW_VB // HEAD_DIM + g, prev(b, t))),
                  pl.BlockSpec((HPG,) + bs_tab.shape[1:], lambda b, g, t: (g, 0, 0))],
        out_specs=pl.BlockSpec((gw, wide), lambda b, g, t: (g, b * nt + t)),
        out_shape=jax.ShapeDtypeStruct((SWA_HEADS * HEAD_DIM, n), BF16),
        scratch_shapes=[pltpu.VMEM((tq, HPG * tq), F32) for _ in range(2 * SWA_SUB)],
        compiler_params=_params(("arbitrary", "arbitrary", "arbitrary")),
    )(sinks, zt, zk, zk, zt, zt, bs_tab)


def _fox_keys_kernel(fz_ref, bf_ref, k_ref, o_ref, c_ref):
    seq = fz_ref.shape[0]
    ch = LANES
    logf = jax.nn.log_sigmoid(fz_ref[...] + bf_ref[...])
    tri = (lax.broadcasted_iota(jnp.int32, (ch, ch), 0) >= lax.broadcasted_iota(jnp.int32, (ch, ch), 1)).astype(F32)
    carry = jnp.zeros((1, LANES), F32)
    for i in range(seq // ch):
        c = jnp.dot(tri, logf[i * ch:(i + 1) * ch, :], preferred_element_type=F32,
                    precision=lax.Precision.HIGHEST) + carry
        c_ref[i * ch:(i + 1) * ch, :] = c
        carry = c[ch - 1:ch, :]
    neg = c_ref[...] * (-LOG2E)
    hi = neg.astype(BF16)
    r1 = neg - hi.astype(F32)
    mid = r1.astype(BF16)
    lo = (r1 - mid.astype(F32)).astype(BF16)
    parts = jnp.concatenate([hi, mid, lo], axis=1)
    wide = FOX_HEADS * LANES
    row = lax.broadcasted_iota(jnp.int32, (3 * LANES, wide), 0)
    col = lax.broadcasted_iota(jnp.int32, (3 * LANES, wide), 1)
    head, lane = jnp.right_shift(col, LANES.bit_length() - 1), jnp.bitwise_and(col, LANES - 1)
    term, src = jnp.right_shift(row, LANES.bit_length() - 1), jnp.bitwise_and(row, LANES - 1)
    place = (src == COL_FC + head) & (lane == LANE_C + term)
    bias = _dot(parts, place.astype(BF16)).astype(BF16)
    key_lane = jnp.bitwise_and(lax.broadcasted_iota(jnp.int32, (seq, wide), 1), LANES - 1)
    o_ref[...] = jnp.where(key_lane < HEAD_DIM, k_ref[...], bias)


def _fox_keys(fz, layer, bf_rows, zk, bsz, seq):
    wide = FOX_HEADS * LANES
    return pl.pallas_call(
        _fox_keys_kernel,
        name="fox_keys",
        grid=(bsz,),
        in_specs=[pl.BlockSpec((seq, LANES), lambda b: (b, 0)),
                  _layer_spec(bf_rows, layer),
                  pl.BlockSpec((seq, wide), lambda b: (b, CB_KC))],
        out_specs=pl.BlockSpec((seq, wide), lambda b: (b, 0)),
        out_shape=jax.ShapeDtypeStruct((bsz * seq, wide), BF16),
        scratch_shapes=[pltpu.VMEM((seq, LANES), F32)],
        compiler_params=_params(("arbitrary",)),
    )(fz, bf_rows, zk)


FOX_T = 512
FOX_HPS = 4


def _fox_kernel(q_ref, k_ref, vt_ref, o_ref, *s_refs):
    qi = pl.program_id(2)
    t = q_ref.shape[1]
    tk = t // 2
    ones_rows = (lax.broadcasted_iota(jnp.int32, (HEAD_DIM, t), 0) < 3).astype(BF16)
    rows = lambda hh: slice(hh * HEAD_DIM, (hh + 1) * HEAD_DIM)
    q_aug = [jnp.concatenate([q_ref[rows(hh), :], ones_rows], axis=0) for hh in range(FOX_HPS)]

    def scores(tile, slot):
        k0 = pl.multiple_of(tile * tk, tk)
        for hh in range(FOX_HPS):
            s_refs[2 * hh + slot][...] = _dot(k_ref[pl.ds(k0, tk), hh * LANES:(hh + 1) * LANES], q_aug[hh])

    def consume(slot, tile, carry, first_masked_query=None):
        k0 = pl.multiple_of(tile * tk, tk)
        out = []
        for hh in range(FOX_HPS):
            s = s_refs[2 * hh + slot][...]
            if first_masked_query is not None:
                key = lax.broadcasted_iota(jnp.int32, (tk, t), 0) + first_masked_query
                s = jnp.where(key > lax.broadcasted_iota(jnp.int32, (tk, t), 1), NEG, s)
            out.append(_softmax_update(s, *carry[hh], vt_ref[rows(hh), pl.ds(k0, tk)]))
        return tuple(out)

    def body(j, carry):
        scores(2 * j + 1, 1)
        carry = consume(0, 2 * j, carry)
        scores(2 * j + 2, 0)
        return consume(1, 2 * j + 1, carry)

    scores(0, 0)
    carry = lax.fori_loop(0, qi, body, tuple(_softmax_init(t) for _ in range(FOX_HPS)))
    scores(2 * qi + 1, 1)
    carry = consume(0, 2 * qi, carry, 0)
    carry = consume(1, 2 * qi + 1, carry, tk)
    for hh in range(FOX_HPS):
        o_ref[rows(hh), :] = _softmax_finish(carry[hh][1]).astype(BF16)


def _fox(zt, kaug, bsz, seq):
    t = min(FOX_T, seq)
    nq = seq // t
    n = bsz * seq
    hw = FOX_HPS * HEAD_DIM
    assert ROW_QC % hw == 0 and ROW_VC % hw == 0
    return pl.pallas_call(
        _fox_kernel,
        name="fox_attn",
        grid=(bsz, FOX_HEADS // FOX_HPS, nq),
        in_specs=[pl.BlockSpec((hw, t), lambda b, h, q: (ROW_QC // hw + h, b * nq + q)),
                  pl.BlockSpec((seq, FOX_HPS * LANES), lambda b, h, q: (b, h)),
                  pl.BlockSpec((hw, seq), lambda b, h, q: (ROW_VC // hw + h, b))],
        out_specs=pl.BlockSpec((hw, t), lambda b, h, q: (h, b * nq + q)),
        out_shape=jax.ShapeDtypeStruct((FOX_HEADS * HEAD_DIM, n), BF16),
        scratch_shapes=[pltpu.VMEM((t // 2, t), F32) for _ in range(2 * FOX_HPS)],
        compiler_params=_params(("arbitrary", "arbitrary", "arbitrary")),
    )(zt, kaug, zt)


MERGE_TM = 1024


def _merge_ln_kernel(x_ref, xb_ref, oa_ref, ob_ref, oc_ref, wg_ref, bg_ref, wbr_ref, wo_ref, g_ref, b_ref,
                     o_ref, obf_ref):
    d = x_ref.shape[1]
    xb = xb_ref[...]
    merged = None
    for i, br_ref in enumerate((oa_ref, ob_ref, oc_ref)):
        gate = jax.nn.sigmoid(_dot(xb, wg_ref[:, i * d:(i + 1) * d]) + bg_ref[:, i * d:(i + 1) * d])
        term = gate * _dot_tn(br_ref[...], wbr_ref[i])
        merged = term if merged is None else merged + term
    mix = _dot(merged.astype(BF16), wo_ref[...])
    out = _layer_norm(DN_ALPHA * x_ref[...] + mix, g_ref[...], b_ref[...])
    o_ref[...] = out
    obf_ref[...] = out.astype(BF16)


def _merge_ln(x, xb, ota, otb, otc, layer, wg, bg, wbr, wo, g, b):
    n, d = x.shape
    tm = min(MERGE_TM, n)
    w = ota.shape[0]
    tok = pl.BlockSpec((tm, d), lambda i: (i, 0))
    feat = pl.BlockSpec((w, tm), lambda i: (0, i))
    return pl.pallas_call(
        _merge_ln_kernel,
        name="merge_ln",
        grid=(n // tm,),
        in_specs=[tok, tok, feat, feat, feat]
        + [_layer_spec(a, layer, single_buffer=True) for a in (wg, bg, wbr, wo, g, b)],
        out_specs=[tok, tok],
        out_shape=[jax.ShapeDtypeStruct((n, d), F32), jax.ShapeDtypeStruct((n, d), BF16)],
        compiler_params=_params(("arbitrary",)),
    )(x, xb, ota, otb, otc, wg, bg, wbr, wo, g, b)


def _split_w_in(w_in):
    sizes = [NSA_HEADS * HEAD_DIM, 6 * NSA_GROUPS * HEAD_DIM, N_GA, SWA_HEADS * HEAD_DIM,
             SWA_GROUPS * HEAD_DIM, SWA_GROUPS * HEAD_DIM, FOX_HEADS * HEAD_DIM, FOX_HEADS * HEAD_DIM,
             FOX_HEADS * HEAD_DIM, FOX_HEADS]
    q_a, kv_a, g_a, q_b, k_b, v_b, q_c, k_c, v_c, f_c = jnp.split(w_in, np.cumsum(sizes)[:-1].tolist(), axis=-1)
    k_cmp, v_cmp, k_slc, v_slc, k_win, v_win = jnp.split(kv_a, 6, axis=-1)
    depth, d = w_in.shape[0], w_in.shape[1]

    w_feat_t = jnp.swapaxes(jnp.concatenate([q_a, q_b, q_c, v_c, v_slc, v_win, v_b], axis=-1), 1, 2)
    w_k = jnp.concatenate([k_c, k_slc, k_win, k_b], axis=-1)
    w_cmp = jnp.concatenate([k_cmp, v_cmp], axis=-1)
    small = jnp.concatenate([g_a, f_c], axis=-1)
    w_small = jnp.concatenate([small, jnp.zeros((depth, d, LANES - ROWS_SMALL), small.dtype)], axis=-1)
    w_small_t = jnp.swapaxes(small, 1, 2)
    return w_feat_t, w_k, w_cmp, w_small, w_small_t


def kernel(x, rel_bias, ln1_g, ln1_b, ffn1_w1, ffn1_w2, w_in, cmp_pe_k, cmp_pe_v, cmp_k_w1, cmp_k_w2, cmp_v_w1,
           cmp_v_w2, swa_sinks, fox_b_f, w_br_a, w_br_b, w_br_c, w_gate, b_gate, w_out, ln2_g, ln2_b, ffn2_w1,
           ffn2_w2, ln3_g, ln3_b):
    bsz, seq, d = x.shape
    n = bsz * seq
    depth = w_in.shape[0]
    nch = seq // CMP_STRIDE
    tabs = _static_tables(seq)
    onehot = jnp.asarray(tabs["onehot"], BF16)
    band0 = jnp.asarray(tabs["band0"])
    u_tab, bn_tab, bs_tab = _bias_tables(rel_bias, tabs)

    bf = lambda w: w.astype(BF16)
    w_feat_t, w_k, w_cmp, w_small, w_small_t = [bf(w) for w in _split_w_in(w_in)]
    f1w1, f1w2, f2w1, f2w2 = bf(ffn1_w1), bf(ffn1_w2), bf(ffn2_w1), bf(ffn2_w2)
    ck1, ck2, cv1 = bf(cmp_k_w1), bf(cmp_k_w2), bf(cmp_v_w1)
    cv2t = bf(jnp.swapaxes(cmp_v_w2, 1, 2))
    pek = cmp_pe_k.reshape(depth, 1, CMP_BLOCK * HEAD_DIM)
    pev = cmp_pe_v.reshape(depth, 1, CMP_BLOCK * HEAD_DIM)
    wg, wo = bf(w_gate), bf(w_out)
    wbr = bf(jnp.stack([w_br_a, w_br_b, w_br_c], axis=1))
    bf_rows = jnp.zeros((depth, 1, LANES), F32).at[:, 0, COL_FC:COL_FC + FOX_HEADS].set(fox_b_f)
    rows = lambda v: v.reshape(depth, 1, -1)
    ln1, ln2, ln3 = (rows(ln1_g), rows(ln1_b)), (rows(ln2_g), rows(ln2_b)), (rows(ln3_g), rows(ln3_b))
    bg = rows(b_gate)

    h = x.reshape(n, d)
    for l in range(depth):
        h, hb = _ffn_ln(h, l, f1w1, f1w2, *ln1)
        zt, zk, zc, fz, gt = _project(hb, l, w_feat_t, w_k, w_cmp, w_small, w_small_t)
        zc4 = zc.reshape(bsz, nch, CMP_STRIDE, 2 * NSA_GROUPS, HEAD_DIM).transpose(0, 3, 1, 2, 4)
        zc4 = zc4.reshape(bsz, 2 * NSA_GROUPS, nch, CMP_STRIDE * HEAD_DIM)
        kc, vct = _compress(zc4, l, pek, pev, ck1, ck2, cv1, cv2t)
        selb, part_c = _nsa_select(zt, gt, kc, vct, u_tab, bsz, seq)
        ot_a = _nsa(zt, zk, gt, selb, part_c, onehot, bn_tab, band0, bsz, seq)
        ot_b = _swa(zt, zk, l, swa_sinks, bs_tab, bsz, seq)
        kaug = _fox_keys(fz, l, bf_rows, zk, bsz, seq)
        ot_c = _fox(zt, kaug, bsz, seq)
        h, hb = _merge_ln(h, hb, ot_a, ot_b, ot_c, l, wg, bg, wbr, wo, *ln2)
        h, hb = _ffn_ln(h, l, f2w1, f2w2, *ln3)
    return h.reshape(bsz, seq, d)
```
